```python
import math
import jax, jax.numpy as jnp
from jax import lax
import numpy as np

D_MODEL = 2048
BATCH = 2
SEQ = 4096
DEPTH = 1

ATTN_HEAD_DIM = 128
ATTN_HEADS = D_MODEL // ATTN_HEAD_DIM
ATTN_WIDTH = ATTN_HEADS * ATTN_HEAD_DIM
MOBA_BLOCK = 256
MOBA_TOPK = 3
QUERY_CHUNK = 32
SSM_EXPAND = 2
SSM_INNER = SSM_EXPAND * D_MODEL
SSM_HEAD_DIM = 64
SSM_HEADS = SSM_INNER // SSM_HEAD_DIM
SSM_GROUPS = 8
SSM_STATE = 128
SSM_CONV = 4
SSM_CHUNK = 128
SSM_CONV_DIM = SSM_INNER + 2 * SSM_GROUPS * SSM_STATE
N_BRANCHES = 2
D_FF = 4 * D_MODEL
IN_SPLITS = (ATTN_WIDTH, ATTN_WIDTH, ATTN_WIDTH, SSM_INNER, SSM_CONV_DIM, SSM_HEADS, N_BRANCHES * D_MODEL)
IN_COLS = 3 * ATTN_WIDTH + SSM_INNER + SSM_CONV_DIM + SSM_HEADS + N_BRANCHES * D_MODEL
EPS = 1e-6
NEG_INF = -1e30

kernel_name = "hybrid_moba_ssd_gated_block"


def rms_norm(x, w):
    xf = x.astype(jnp.float32)
    y = xf * lax.rsqrt(jnp.mean(jnp.square(xf), axis=-1, keepdims=True) + EPS)
    return (y * w.astype(jnp.float32)).astype(x.dtype)


def alibi_slopes(n_heads):
    return jnp.exp2(-8.0 * jnp.arange(1, n_heads + 1, dtype=jnp.float32) / n_heads)


def moba_attention(q, k, v, slopes):
    bsz, n_h, seq, dh = q.shape
    nb = -(-seq // MOBA_BLOCK)
    pad = nb * MOBA_BLOCK - seq
    kp = jnp.pad(k, ((0, 0), (0, 0), (0, pad), (0, 0)))
    vp = jnp.pad(v, ((0, 0), (0, 0), (0, pad), (0, 0)))
    k_blocks = kp.reshape(bsz, n_h, nb, MOBA_BLOCK, dh)
    v_blocks = vp.reshape(bsz, n_h, nb, MOBA_BLOCK, dh)
    k_mean = jnp.mean(k_blocks.astype(jnp.float32), axis=3)
    n_sel = max(1, min(MOBA_TOPK, nb - 1))
    scale = dh ** -0.5
    gather_blocks = jax.vmap(jax.vmap(lambda blocks, idx: blocks[idx]))
    blk_pos = jnp.arange(MOBA_BLOCK)

    def chunk_fn(c):
        q0 = c * QUERY_CHUNK
        qc = lax.dynamic_slice_in_dim(q, q0, QUERY_CHUNK, axis=2)
        t = q0 + jnp.arange(QUERY_CHUNK)
        own = q0 // MOBA_BLOCK
        gate = jnp.einsum('bhqd,bhnd->bhqn', qc.astype(jnp.float32), k_mean)
        gate = jnp.where(jnp.arange(nb) < own, gate, NEG_INF)
        _, sel = lax.top_k(gate, n_sel)
        sel_valid = sel < own
        kg = gather_blocks(k_blocks, sel)
        vg = gather_blocks(v_blocks, sel)
        s_sel = jnp.einsum('bhqd,bhqjkd->bhqjk', qc, kg).astype(jnp.float32) * scale
        pos_sel = sel[..., None] * MOBA_BLOCK + blk_pos
        dist_sel = (t[:, None, None] - pos_sel).astype(jnp.float32)
        s_sel = s_sel - slopes[:, None, None, None] * dist_sel
        s_sel = jnp.where(sel_valid[..., None], s_sel, NEG_INF)
        ko = lax.dynamic_slice_in_dim(kp, own * MOBA_BLOCK, MOBA_BLOCK, axis=2)
        vo = lax.dynamic_slice_in_dim(vp, own * MOBA_BLOCK, MOBA_BLOCK, axis=2)
        s_own = jnp.einsum('bhqd,bhkd->bhqk', qc, ko).astype(jnp.float32) * scale
        dist_own = t[:, None] - (own * MOBA_BLOCK + blk_pos)[None, :]
        s_own = jnp.where(dist_own >= 0, s_own - slopes[:, None, None] * dist_own.astype(jnp.float32), NEG_INF)
        logits = jnp.concatenate([s_sel.reshape(bsz, n_h, QUERY_CHUNK, n_sel * MOBA_BLOCK), s_own], axis=-1)
        p = jax.nn.softmax(logits, axis=-1)
        p_sel = p[..., :n_sel * MOBA_BLOCK].reshape(bsz, n_h, QUERY_CHUNK, n_sel, MOBA_BLOCK)
        p_own = p[..., n_sel * MOBA_BLOCK:]
        out = (jnp.einsum('bhqjk,bhqjkd->bhqd', p_sel, vg.astype(jnp.float32))
               + jnp.einsum('bhqk,bhkd->bhqd', p_own, vo.astype(jnp.float32)))
        return out.astype(q.dtype)

    outs = lax.map(chunk_fn, jnp.arange(seq // QUERY_CHUNK))
    return outs.transpose(1, 2, 0, 3, 4).reshape(bsz, n_h, seq, dh)


def causal_depthwise_conv(u, w, b):
    width = w.shape[0]
    out = lax.conv_general_dilated(
        u, w[:, None, :], window_strides=(1,), padding=[(width - 1, 0)],
        dimension_numbers=('NWC', 'WIO', 'NWC'), feature_group_count=u.shape[-1])
    return out + b


def ssd_chunked(x, dt, a, b_in, c_in):
    bsz, seq, n_h, p = x.shape
    g, n = b_in.shape[2], b_in.shape[3]
    hg = n_h // g
    ln = SSM_CHUNK
    nc = seq // ln
    xf = (x.astype(jnp.float32) * dt[..., None]).reshape(bsz, nc, ln, g, hg, p)
    la = (dt * a).reshape(bsz, nc, ln, g, hg)
    a_cum = jnp.cumsum(la, axis=2)
    bc = b_in.astype(jnp.float32).reshape(bsz, nc, ln, g, n)
    cc = c_in.astype(jnp.float32).reshape(bsz, nc, ln, g, n)
    seg = a_cum[:, :, :, None] - a_cum[:, :, None]
    causal = jnp.tril(jnp.ones((ln, ln), dtype=bool))[None, None, :, :, None, None]
    decay = jnp.exp(jnp.where(causal, seg, -jnp.inf))
    cb = jnp.einsum('bclgn,bcsgn->bclsg', cc, bc)
    y_diag = jnp.einsum('bclsgh,bcsghp->bclghp', cb[..., None] * decay, xf)
    decay_to_end = jnp.exp(a_cum[:, :, -1:] - a_cum)
    states = jnp.einsum('bclgn,bclghp->bcghpn', bc, xf * decay_to_end[..., None])
    chunk_decay = jnp.exp(a_cum[:, :, -1])

    def step(h, inp):
        st, dcy = inp
        return h * dcy[..., None, None] + st, h

    h0 = jnp.zeros((bsz, g, hg, p, n), jnp.float32)
    _, prev = lax.scan(step, h0, (states.swapaxes(0, 1), chunk_decay.swapaxes(0, 1)))
    prev = prev.swapaxes(0, 1)
    y_off = jnp.einsum('bclgn,bcghpn->bclghp', cc, prev) * jnp.exp(a_cum)[..., None]
    return (y_diag + y_off).reshape(bsz, seq, n_h, p)


def mamba2_branch(z, xbc, dt_raw, conv_w, conv_b, dt_bias, a_log, d_skip, norm_w):
    bsz, seq, _ = xbc.shape
    xbc = jax.nn.silu(causal_depthwise_conv(xbc, conv_w, conv_b))
    xs = xbc[..., :SSM_INNER].reshape(bsz, seq, SSM_HEADS, SSM_HEAD_DIM)
    b_in = xbc[..., SSM_INNER:SSM_INNER + SSM_GROUPS * SSM_STATE].reshape(bsz, seq, SSM_GROUPS, SSM_STATE)
    c_in = xbc[..., SSM_INNER + SSM_GROUPS * SSM_STATE:].reshape(bsz, seq, SSM_GROUPS, SSM_STATE)
    dt = jax.nn.softplus(dt_raw.astype(jnp.float32) + dt_bias.astype(jnp.float32))
    a = -jnp.exp(a_log.astype(jnp.float32))
    y = ssd_chunked(xs, dt, a, b_in, c_in) + xs.astype(jnp.float32) * d_skip.astype(jnp.float32)[:, None]
    y = y.reshape(bsz, seq, SSM_INNER) * jax.nn.silu(z.astype(jnp.float32))
    yg = y.reshape(bsz, seq, SSM_GROUPS, SSM_INNER // SSM_GROUPS)
    yg = yg * lax.rsqrt(jnp.mean(jnp.square(yg), axis=-1, keepdims=True) + EPS)
    y = yg.reshape(bsz, seq, SSM_INNER) * norm_w.astype(jnp.float32)
    return y.astype(z.dtype)


def setup_inputs(seed: int = 0) -> dict:
    key = jax.random.key(seed)
    ks = jax.random.split(key, 18)
    f32 = jnp.float32

    def normal(k, shape, scale):
        return jax.random.normal(k, shape, f32) * scale

    x = normal(ks[0], (BATCH, SEQ, D_MODEL), 1.0)
    mix_norm_w = 1.0 + normal(ks[1], (DEPTH, D_MODEL), 0.02)
    w_in = normal(ks[2], (DEPTH, D_MODEL, IN_COLS), D_MODEL ** -0.5)
    q_norm_w = 1.0 + normal(ks[3], (DEPTH, ATTN_HEAD_DIM), 0.02)
    k_norm_w = 1.0 + normal(ks[4], (DEPTH, ATTN_HEAD_DIM), 0.02)
    conv_w = normal(ks[5], (DEPTH, SSM_CONV, SSM_CONV_DIM), SSM_CONV ** -0.5)
    conv_b = normal(ks[6], (DEPTH, SSM_CONV_DIM), 0.01)
    dt0 = jnp.exp(jax.random.uniform(ks[7], (DEPTH, SSM_HEADS), f32, math.log(1e-3), math.log(1e-1)))
    dt_bias = dt0 + jnp.log(-jnp.expm1(-dt0))
    a_log = jnp.log(jax.random.uniform(ks[8], (DEPTH, SSM_HEADS), f32, 1.0, 16.0))
    d_skip = 1.0 + normal(ks[9], (DEPTH, SSM_HEADS), 0.02)
    ssm_norm_w = 1.0 + normal(ks[10], (DEPTH, SSM_INNER), 0.02)
    w_attn_out = normal(ks[11], (DEPTH, ATTN_WIDTH, D_MODEL), ATTN_WIDTH ** -0.5)
    w_ssm_out = normal(ks[12], (DEPTH, SSM_INNER, D_MODEL), SSM_INNER ** -0.5)
    w_out = normal(ks[13], (DEPTH, D_MODEL, D_MODEL), D_MODEL ** -0.5)
    mlp_norm_w = 1.0 + normal(ks[14], (DEPTH, D_MODEL), 0.02)
    w_up = normal(ks[15], (DEPTH, D_MODEL, D_FF), D_MODEL ** -0.5)
    w_down = normal(ks[16], (DEPTH, D_FF, D_MODEL), D_FF ** -0.5)
    return {"x": x, "mix_norm_w": mix_norm_w, "w_in": w_in, "q_norm_w": q_norm_w, "k_norm_w": k_norm_w,
            "conv_w": conv_w, "conv_b": conv_b, "dt_bias": dt_bias, "a_log": a_log, "d_skip": d_skip,
            "ssm_norm_w": ssm_norm_w, "w_attn_out": w_attn_out, "w_ssm_out": w_ssm_out, "w_out": w_out,
            "mlp_norm_w": mlp_norm_w, "w_up": w_up, "w_down": w_down}


def reference(x, mix_norm_w, w_in, q_norm_w, k_norm_w, conv_w, conv_b, dt_bias, a_log, d_skip,
              ssm_norm_w, w_attn_out, w_ssm_out, w_out, mlp_norm_w, w_up, w_down):
    bsz, seq, _ = x.shape
    slopes = alibi_slopes(ATTN_HEADS)
    split_points = []
    acc = 0
    for width in IN_SPLITS[:-1]:
        acc += width
        split_points.append(acc)
    h = x
    for layer in range(DEPTH):
        hn = rms_norm(h, mix_norm_w[layer])
        proj = hn @ w_in[layer]
        q, k, v, z, xbc, dt_raw, gate_logits = jnp.split(proj, split_points, axis=-1)
        q = rms_norm(q.reshape(bsz, seq, ATTN_HEADS, ATTN_HEAD_DIM), q_norm_w[layer]).transpose(0, 2, 1, 3)
        k = rms_norm(k.reshape(bsz, seq, ATTN_HEADS, ATTN_HEAD_DIM), k_norm_w[layer]).transpose(0, 2, 1, 3)
        v = v.reshape(bsz, seq, ATTN_HEADS, ATTN_HEAD_DIM).transpose(0, 2, 1, 3)
        attn = moba_attention(q, k, v, slopes).transpose(0, 2, 1, 3).reshape(bsz, seq, ATTN_WIDTH)
        y_attn = attn @ w_attn_out[layer]
        ssm = mamba2_branch(z, xbc, dt_raw, conv_w[layer], conv_b[layer], dt_bias[layer], a_log[layer],
                            d_skip[layer], ssm_norm_w[layer])
        y_ssm = ssm @ w_ssm_out[layer]
        gates = jax.nn.sigmoid(gate_logits.astype(jnp.float32)).astype(h.dtype)
        g_attn, g_ssm = gates[..., :D_MODEL], gates[..., D_MODEL:]
        h = h + (g_attn * y_attn + g_ssm * y_ssm) @ w_out[layer]
        u = rms_norm(h, mlp_norm_w[layer]) @ w_up[layer]
        h = h + jnp.square(jax.nn.relu(u)) @ w_down[layer]
    return h
```

```python
import functools

import jax
import jax.numpy as jnp
from jax import lax
from jax.experimental import pallas as pl
from jax.experimental.pallas import tpu as pltpu

D_MODEL = 2048
ATTN_HEAD_DIM = 128
ATTN_HEADS = D_MODEL // ATTN_HEAD_DIM
ATTN_WIDTH = ATTN_HEADS * ATTN_HEAD_DIM
MOBA_BLOCK = 256
MOBA_TOPK = 3
SSM_INNER = 2 * D_MODEL
SSM_HEAD_DIM = 64
SSM_HEADS = SSM_INNER // SSM_HEAD_DIM
SSM_GROUPS = 8
SSM_STATE = 128
SSM_CONV = 4
SSM_CHUNK = 128
SSM_BC = SSM_GROUPS * SSM_STATE
SSM_CONV_DIM = SSM_INNER + 2 * SSM_BC
GROUP_WIDTH = SSM_INNER // SSM_GROUPS
D_FF = 4 * D_MODEL
EPS = 1e-6
NEG_INF = -1e30

LANES = 128
CONV_HALO = 8
VMEM_LIMIT = 56 * 1024 * 1024

F32 = jnp.float32
BF16 = jnp.bfloat16

_NT = (((1,), (1,)), ((), ()))


def _params(*sem):
    return pltpu.CompilerParams(dimension_semantics=sem, vmem_limit_bytes=VMEM_LIMIT)


def _dot(a, b):
    return jnp.dot(a, b, preferred_element_type=F32)


def _dot_nt(a, b):
    return lax.dot_general(a, b, _NT, preferred_element_type=F32)


def _rmsnorm_kernel(x_ref, w_ref, o_ref):
    x = x_ref[...]
    ms = jnp.mean(x * x, axis=-1, keepdims=True)
    o_ref[...] = (x * lax.rsqrt(ms + EPS) * w_ref[...]).astype(o_ref.dtype)


def _rmsnorm(x, w, tm=512):
    t, d = x.shape
    return pl.pallas_call(
        _rmsnorm_kernel,
        grid=(t // tm,),
        in_specs=[pl.BlockSpec((tm, d), lambda i: (i, 0)), pl.BlockSpec((1, d), lambda i: (0, 0))],
        out_specs=pl.BlockSpec((tm, d), lambda i: (i, 0)),
        out_shape=jax.ShapeDtypeStruct((t, d), BF16),
        compiler_params=_params("parallel"),
        name="rmsnorm",
    )(x, w.reshape(1, d))


def _proj_kernel(a_ref, w_ref, o_ref):
    o_ref[...] = _dot(a_ref[...], w_ref[...]).astype(o_ref.dtype)


def _proj_t_kernel(a_ref, wt_ref, o_ref):
    o_ref[...] = _dot_nt(wt_ref[...], a_ref[...]).astype(o_ref.dtype)


def _proj_knorm_kernel(a_ref, w_ref, nw_ref, o_ref):
    acc = _dot(a_ref[...], w_ref[...])
    nw = nw_ref[...]
    for h in range(acc.shape[1] // ATTN_HEAD_DIM):
        slab = acc[:, h * ATTN_HEAD_DIM:(h + 1) * ATTN_HEAD_DIM]
        ms = jnp.mean(slab * slab, axis=-1, keepdims=True)
        o_ref[:, h * ATTN_HEAD_DIM:(h + 1) * ATTN_HEAD_DIM] = (slab * lax.rsqrt(ms + EPS) * nw).astype(o_ref.dtype)


def _proj_qnorm_t_kernel(a_ref, wt_ref, nw_ref, o_ref):
    acc = _dot_nt(wt_ref[...], a_ref[...])
    nw = nw_ref[...]
    for h in range(acc.shape[0] // ATTN_HEAD_DIM):
        slab = acc[h * ATTN_HEAD_DIM:(h + 1) * ATTN_HEAD_DIM, :]
        ms = jnp.mean(slab * slab, axis=0, keepdims=True)
        o_ref[h * ATTN_HEAD_DIM:(h + 1) * ATTN_HEAD_DIM, :] = (slab * lax.rsqrt(ms + EPS) * nw).astype(o_ref.dtype)


def _project(a, w, out_dtype, *, transposed=False, norm_w=None, tm=1024, tn=512, name="proj"):
    t, k = a.shape
    n = w.shape[0] if transposed else w.shape[1]
    tn = min(tn, n)
    a_spec = pl.BlockSpec((tm, k), lambda i, j: (i, 0))
    if transposed:
        w_spec = pl.BlockSpec((tn, k), lambda i, j: (j, 0))
        o_spec = pl.BlockSpec((tn, tm), lambda i, j: (j, i))
        o_shape = (n, t)
    else:
        w_spec = pl.BlockSpec((k, tn), lambda i, j: (0, j))
        o_spec = pl.BlockSpec((tm, tn), lambda i, j: (i, j))
        o_shape = (t, n)
    in_specs = [a_spec, w_spec]
    args = [a, w]
    if norm_w is None:
        body = _proj_t_kernel if transposed else _proj_kernel
    else:
        if transposed:
            body = _proj_qnorm_t_kernel
            in_specs.append(pl.BlockSpec((ATTN_HEAD_DIM, 1), lambda i, j: (0, 0)))
            args.append(norm_w.reshape(ATTN_HEAD_DIM, 1))
        else:
            body = _proj_knorm_kernel
            in_specs.append(pl.BlockSpec((1, ATTN_HEAD_DIM), lambda i, j: (0, 0)))
            args.append(norm_w.reshape(1, ATTN_HEAD_DIM))
    return pl.pallas_call(
        body,
        grid=(t // tm, n // tn),
        in_specs=in_specs,
        out_specs=o_spec,
        out_shape=jax.ShapeDtypeStruct(o_shape, out_dtype),
        compiler_params=_params("parallel", "parallel"),
        name=name,
    )(*args)


def _dt_kernel(a_ref, w_ref, wt_ref, o_ref, ot_ref):
    a = a_ref[...]
    o_ref[...] = _dot(a, w_ref[...])
    ot_ref[...] = _dot_nt(wt_ref[...], a)


def _project_dt(a, w, wt, tm=1024):
    t, k = a.shape
    n = w.shape[1]
    return pl.pallas_call(
        _dt_kernel,
        grid=(t // tm,),
        in_specs=[pl.BlockSpec((tm, k), lambda i: (i, 0)),
                  pl.BlockSpec((k, n), lambda i: (0, 0)),
                  pl.BlockSpec((n, k), lambda i: (0, 0))],
        out_specs=[pl.BlockSpec((tm, n), lambda i: (i, 0)), pl.BlockSpec((n, tm), lambda i: (0, i))],
        out_shape=[jax.ShapeDtypeStruct((t, n), F32), jax.ShapeDtypeStruct((n, t), F32)],
        compiler_params=_params("parallel"),
        name="proj_dt",
    )(a, w, wt)


def _moba_kernel(slopes_ref, qt_ref, k_ref, vt_ref, o_ref, kmh_ref, kml_ref, rel_ref, sel_ref):
    h = pl.program_id(1)
    i = pl.program_id(2)
    blk = MOBA_BLOCK
    seq = k_ref.shape[0]
    nb = seq // blk
    slope = slopes_ref[h]
    scale = ATTN_HEAD_DIM ** -0.5

    @pl.when(i == 0)
    def _():
        r = lax.broadcasted_iota(jnp.int32, (nb, seq), 0)
        c = lax.broadcasted_iota(jnp.int32, (nb, seq), 1)
        lo = r * blk
        pool = jnp.where((c >= lo) & (c < lo + blk), 1.0 / blk, 0.0).astype(BF16)
        kmean = _dot(pool, k_ref[...])
        hi = kmean.astype(BF16)
        kmh_ref[...] = hi
        kml_ref[...] = (kmean - hi.astype(F32)).astype(BF16)
        kl = lax.broadcasted_iota(jnp.int32, (blk, blk), 0)
        ql = lax.broadcasted_iota(jnp.int32, (blk, blk), 1)
        rel_ref[...] = slope * (ql - kl).astype(F32)

    qt = qt_ref[...]

    gate = _dot(kmh_ref[...], qt) + _dot(kml_ref[...], qt)
    jidx = lax.broadcasted_iota(jnp.int32, (nb, blk), 0)
    cnt = jnp.zeros((nb, blk), F32)
    for jp in range(nb):
        row = gate[jp:jp + 1, :]
        beats = (row > gate) | ((row == gate) & (jidx > jp))
        cnt = cnt + jnp.where(beats, (jp < i).astype(F32), 0.0)
    sel_ref[...] = jnp.where((cnt < MOBA_TOPK) & (jidx < i), 1.0, 0.0)

    def scores(j):
        kb = k_ref[pl.ds(pl.multiple_of(j * blk, blk), blk), :]
        s = _dot(kb, qt) * scale
        return s - rel_ref[...] - slope * ((i - j) * blk).astype(F32)

    def values(j):
        return vt_ref[:, pl.ds(pl.multiple_of(j * blk, blk), blk)]

    kl = lax.broadcasted_iota(jnp.int32, (blk, blk), 0)
    ql = lax.broadcasted_iota(jnp.int32, (blk, blk), 1)
    s = jnp.where(kl <= ql, scores(i), NEG_INF)
    m0 = jnp.max(s, axis=0, keepdims=True)
    p = jnp.exp(s - m0)
    l0 = jnp.sum(p, axis=0, keepdims=True)
    acc0 = _dot(values(i), p.astype(BF16))

    def body(j, carry):
        m, l, acc = carry
        keep = sel_ref[pl.ds(j, 1), :] > 0.0
        s = jnp.where(keep, scores(j), NEG_INF)
        m_new = jnp.maximum(m, jnp.max(s, axis=0, keepdims=True))
        alpha = jnp.exp(m - m_new)
        p = jnp.exp(s - m_new)
        l = alpha * l + jnp.sum(p, axis=0, keepdims=True)
        acc = alpha * acc + _dot(values(j), p.astype(BF16))
        return m_new, l, acc

    _, l, acc = lax.fori_loop(0, i, body, (m0, l0, acc0))
    o_ref[...] = (acc / l).T.astype(o_ref.dtype)


def _moba_attention(qt, k, vt, slopes, bsz, seq):
    assert seq % MOBA_BLOCK == 0
    nq = seq // MOBA_BLOCK
    nb = nq
    dh = ATTN_HEAD_DIM
    return pl.pallas_call(
        _moba_kernel,
        grid=(bsz, ATTN_HEADS, nq),
        in_specs=[pl.BlockSpec(memory_space=pltpu.SMEM),
                  pl.BlockSpec((dh, MOBA_BLOCK), lambda b, h, i: (h, b * nq + i)),
                  pl.BlockSpec((seq, dh), lambda b, h, i: (b, h)),
                  pl.BlockSpec((dh, seq), lambda b, h, i: (h, b))],
        out_specs=pl.BlockSpec((MOBA_BLOCK, dh), lambda b, h, i: (b * nq + i, h)),
        out_shape=jax.ShapeDtypeStruct((bsz * seq, ATTN_WIDTH), BF16),
        scratch_shapes=[pltpu.VMEM((nb, dh), BF16), pltpu.VMEM((nb, dh), BF16),
                        pltpu.VMEM((MOBA_BLOCK, MOBA_BLOCK), F32), pltpu.VMEM((nb, MOBA_BLOCK), F32)],
        compiler_params=_params("parallel", "parallel", "arbitrary"),
        name="moba_attention",
    )(slopes, qt, k, vt)


def _split3(x):
    h1 = x.astype(BF16)
    r1 = x - h1.astype(F32)
    h2 = r1.astype(BF16)
    h3 = (r1 - h2.astype(F32)).astype(BF16)
    return h1, h2, h3


def _softplus(x):
    return jnp.maximum(x, 0.0) + jnp.log1p(jnp.exp(-jnp.abs(x)))


def _silu(x):
    return x * jax.nn.sigmoid(x)


def _ssd_kernel(xbc_ref, z_ref, dt_ref, dtt_ref, convw_ref, convb_ref, dtb_ref, dtbt_ref, alog_ref, alogt_ref,
                dskip_ref, normw_ref, o_ref, ext_ref, act_ref, state_ref, y_ref):
    c = pl.program_id(1)
    ln = SSM_CHUNK
    halo = CONV_HALO

    @pl.when(c == 0)
    def _():
        ext_ref[0:halo, :] = jnp.zeros((halo, SSM_CONV_DIM), F32)
        state_ref[...] = jnp.zeros_like(state_ref)

    ext_ref[halo:halo + ln, :] = xbc_ref[...]
    cw = 512
    for c0 in range(0, SSM_CONV_DIM, cw):
        acc = jnp.broadcast_to(convb_ref[:, c0:c0 + cw], (ln, cw))
        for kk in range(SSM_CONV):
            r0 = halo - (SSM_CONV - 1) + kk
            acc = acc + convw_ref[kk:kk + 1, c0:c0 + cw] * ext_ref[r0:r0 + ln, c0:c0 + cw]
        act_ref[:, c0:c0 + cw] = _silu(acc)
    ext_ref[0:halo, :] = ext_ref[ln:ln + halo, :]

    dt = _softplus(dt_ref[...] + dtb_ref[...])
    dtt = _softplus(dtt_ref[...] + dtbt_ref[...])
    la = dt * (-jnp.exp(alog_ref[...]))
    lat = dtt * (-jnp.exp(alogt_ref[...]))
    row = lax.broadcasted_iota(jnp.int32, (ln, ln), 0)
    col = lax.broadcasted_iota(jnp.int32, (ln, ln), 1)
    causal = row >= col
    tri = jnp.where(causal, 1.0, 0.0).astype(BF16)
    trit = jnp.where(row <= col, 1.0, 0.0).astype(BF16)
    a1, a2, a3 = _split3(la)
    acum = _dot(tri, a1) + _dot(tri, a2) + _dot(tri, a3)
    b1, b2, b3 = _split3(lat)
    acumt = _dot(b1, trit) + _dot(b2, trit) + _dot(b3, trit)

    left = lax.broadcasted_iota(jnp.int32, (1, LANES), 1) < SSM_HEAD_DIM
    pairs_per_group = GROUP_WIDTH // LANES

    for g in range(SSM_GROUPS):
        b0 = SSM_INNER + g * SSM_STATE
        c0 = SSM_INNER + SSM_BC + g * SSM_STATE
        bg = act_ref[:, b0:b0 + SSM_STATE]
        cgb = act_ref[:, c0:c0 + SSM_STATE].astype(BF16)
        cb = _dot_nt(cgb, bg.astype(BF16))
        bgt = bg.T
        for pp in range(pairs_per_group):
            p = g * pairs_per_group + pp
            x2 = act_ref[:, p * LANES:(p + 1) * LANES].astype(BF16)
            lhs = []
            eacs = []
            for hh in (2 * p, 2 * p + 1):
                colb = jnp.broadcast_to(acum[:, hh:hh + 1], (ln, ln))
                rowa = acumt[hh:hh + 1, :]
                dtr = dtt[hh:hh + 1, :]
                decay = jnp.exp(jnp.where(causal, colb - rowa, -jnp.inf))
                lhs.append((cb * decay * dtr).astype(BF16))
                wt = dtr * jnp.exp(acumt[hh:hh + 1, ln - 1:ln] - rowa)
                lhs.append((bgt * wt).astype(BF16))
                eacs.append(jnp.exp(colb))
            res = _dot(jnp.concatenate(lhs, axis=0), x2)
            res = jnp.where(left, res[0:2 * ln], res[2 * ln:4 * ln])
            eac = jnp.where(left, eacs[0], eacs[1])
            st = state_ref[p]
            y = res[0:ln] + _dot(cgb, st.astype(BF16)) * eac
            y_ref[:, p * LANES:(p + 1) * LANES] = y
            state_ref[p] = st * eac[ln - 1:ln, :] + res[ln:2 * ln]
        g0 = g * GROUP_WIDTH
        yg = y_ref[:, g0:g0 + GROUP_WIDTH] + act_ref[:, g0:g0 + GROUP_WIDTH] * dskip_ref[:, g0:g0 + GROUP_WIDTH]
        yg = yg * _silu(z_ref[:, g0:g0 + GROUP_WIDTH])
        ms = jnp.mean(yg * yg, axis=-1, keepdims=True)
        o_ref[:, g0:g0 + GROUP_WIDTH] = (yg * lax.rsqrt(ms + EPS) * normw_ref[:, g0:g0 + GROUP_WIDTH]).astype(o_ref.dtype)


def _ssd_branch(xbc, z, dt, dtt, conv_w, conv_b, dt_bias, a_log, d_skip, norm_w, bsz, seq):
    assert seq % SSM_CHUNK == 0
    nc = seq // SSM_CHUNK
    ln = SSM_CHUNK
    hs = SSM_HEADS
    row = lambda b, c: (b * nc + c, 0)
    fixed = lambda b, c: (0, 0)
    return pl.pallas_call(
        _ssd_kernel,
        grid=(bsz, nc),
        in_specs=[pl.BlockSpec((ln, SSM_CONV_DIM), row),
                  pl.BlockSpec((ln, SSM_INNER), row),
                  pl.BlockSpec((ln, hs), row),
                  pl.BlockSpec((hs, ln), lambda b, c: (0, b * nc + c)),
                  pl.BlockSpec((SSM_CONV, SSM_CONV_DIM), fixed),
                  pl.BlockSpec((1, SSM_CONV_DIM), fixed),
                  pl.BlockSpec((1, hs), fixed),
                  pl.BlockSpec((hs, 1), fixed),
                  pl.BlockSpec((1, hs), fixed),
                  pl.BlockSpec((hs, 1), fixed),
                  pl.BlockSpec((1, SSM_INNER), fixed),
                  pl.BlockSpec((1, SSM_INNER), fixed)],
        out_specs=pl.BlockSpec((ln, SSM_INNER), row),
        out_shape=jax.ShapeDtypeStruct((bsz * seq, SSM_INNER), BF16),
        scratch_shapes=[pltpu.VMEM((ln + CONV_HALO, SSM_CONV_DIM), F32),
                        pltpu.VMEM((ln, SSM_CONV_DIM), F32),
                        pltpu.VMEM((SSM_HEADS // 2, SSM_STATE, LANES), F32),
                        pltpu.VMEM((ln, SSM_INNER), F32)],
        compiler_params=_params("arbitrary", "arbitrary"),
        name="ssd",
    )(xbc, z, dt, dtt, conv_w, conv_b.reshape(1, -1), dt_bias.reshape(1, hs), dt_bias.reshape(hs, 1),
      a_log.reshape(1, hs), a_log.reshape(hs, 1),
      jnp.repeat(d_skip, SSM_HEAD_DIM).reshape(1, SSM_INNER), norm_w.reshape(1, SSM_INNER))


def _merge_kernel(attn_ref, ssm_ref, ga_ref, gs_ref, wa_ref, ws_ref, o_ref):
    ya = _dot(attn_ref[...], wa_ref[...])
    ys = _dot(ssm_ref[...], ws_ref[...])
    o_ref[...] = (jax.nn.sigmoid(ga_ref[...]) * ya + jax.nn.sigmoid(gs_ref[...]) * ys).astype(o_ref.dtype)


def _merge(attn, ssm, gate, wa, ws, tm=1024, tn=512):
    t = attn.shape[0]
    n = wa.shape[1]
    nj = n // tn
    return pl.pallas_call(
        _merge_kernel,
        grid=(t // tm, nj),
        in_specs=[pl.BlockSpec((tm, attn.shape[1]), lambda i, j: (i, 0)),
                  pl.BlockSpec((tm, ssm.shape[1]), lambda i, j: (i, 0)),
                  pl.BlockSpec((tm, tn), lambda i, j: (i, j)),
                  pl.BlockSpec((tm, tn), lambda i, j: (i, j + nj)),
                  pl.BlockSpec((wa.shape[0], tn), lambda i, j: (0, j)),
                  pl.BlockSpec((ws.shape[0], tn), lambda i, j: (0, j))],
        out_specs=pl.BlockSpec((tm, tn), lambda i, j: (i, j)),
        out_shape=jax.ShapeDtypeStruct((t, n), BF16),
        compiler_params=_params("parallel", "parallel"),
        name="merge",
    )(attn, ssm, gate, gate, wa, ws)


def _residual_proj_kernel(a_ref, w_ref, r_ref, o_ref):
    o_ref[...] = r_ref[...] + _dot(a_ref[...], w_ref[...])


def _residual_proj(a, w, resid, tm=1024, tn=512):
    t, k = a.shape
    n = w.shape[1]
    return pl.pallas_call(
        _residual_proj_kernel,
        grid=(t // tm, n // tn),
        in_specs=[pl.BlockSpec((tm, k), lambda i, j: (i, 0)),
                  pl.BlockSpec((k, tn), lambda i, j: (0, j)),
                  pl.BlockSpec((tm, tn), lambda i, j: (i, j))],
        out_specs=pl.BlockSpec((tm, tn), lambda i, j: (i, j)),
        out_shape=jax.ShapeDtypeStruct((t, n), F32),
        compiler_params=_params("parallel", "parallel"),
        name="out_proj",
    )(a, w, resid)


def _mlp_kernel(h_ref, nw_ref, wu_ref, wd_ref, o_ref, hn_ref):
    j = pl.program_id(1)

    @pl.when(j == 0)
    def _():
        h = h_ref[...]
        ms = jnp.mean(h * h, axis=-1, keepdims=True)
        hn_ref[...] = (h * lax.rsqrt(ms + EPS) * nw_ref[...]).astype(hn_ref.dtype)
        o_ref[...] = h

    u = _dot(hn_ref[...], wu_ref[...])
    act = jnp.square(jnp.maximum(u, 0.0)).astype(BF16)
    o_ref[...] += _dot(act, wd_ref[...])


def _mlp(h, norm_w, w_up, w_down, tm=512, tf=512):
    t, d = h.shape
    f = w_up.shape[1]
    return pl.pallas_call(
        _mlp_kernel,
        grid=(t // tm, f // tf),
        in_specs=[pl.BlockSpec((tm, d), lambda i, j: (i, 0)),
                  pl.BlockSpec((1, d), lambda i, j: (0, 0)),
                  pl.BlockSpec((d, tf), lambda i, j: (0, j)),
                  pl.BlockSpec((tf, d), lambda i, j: (j, 0))],
        out_specs=pl.BlockSpec((tm, d), lambda i, j: (i, 0)),
        out_shape=jax.ShapeDtypeStruct((t, d), F32),
        scratch_shapes=[pltpu.VMEM((tm, d), BF16)],
        compiler_params=_params("parallel", "arbitrary"),
        name="mlp",
    )(h, norm_w.reshape(1, d), w_up, w_down)


def _layer(h, bsz, seq, mix_norm_w, w_in, q_norm_w, k_norm_w, conv_w, conv_b, dt_bias, a_log, d_skip,
           ssm_norm_w, w_attn_out, w_ssm_out, w_out, mlp_norm_w, w_up, w_down, slopes):
    o_k = ATTN_WIDTH
    o_v = 2 * ATTN_WIDTH
    o_z = 3 * ATTN_WIDTH
    o_x = o_z + SSM_INNER
    o_dt = o_x + SSM_CONV_DIM
    o_g = o_dt + SSM_HEADS
    wq_t = w_in[:, :o_k].T.astype(BF16)
    wk = w_in[:, o_k:o_v].astype(BF16)
    wv_t = w_in[:, o_v:o_z].T.astype(BF16)
    wz = w_in[:, o_z:o_x].astype(BF16)
    wx = w_in[:, o_x:o_dt].astype(BF16)
    wdt = w_in[:, o_dt:o_g].astype(BF16)
    wg = w_in[:, o_g:].astype(BF16)

    hn = _rmsnorm(h, mix_norm_w)
    qt = _project(hn, wq_t, BF16, transposed=True, norm_w=q_norm_w, name="proj_q")
    k = _project(hn, wk, BF16, norm_w=k_norm_w, name="proj_k")
    vt = _project(hn, wv_t, BF16, transposed=True, name="proj_v")
    z = _project(hn, wz, F32, name="proj_z")
    xbc = _project(hn, wx, F32, name="proj_xbc")
    gate = _project(hn, wg, F32, name="proj_gate")
    dt, dtt = _project_dt(hn, wdt, wdt.T)

    attn = _moba_attention(qt, k, vt, slopes, bsz, seq)
    ssm = _ssd_branch(xbc, z, dt, dtt, conv_w, conv_b, dt_bias, a_log, d_skip, ssm_norm_w, bsz, seq)
    merged = _merge(attn, ssm, gate, w_attn_out.astype(BF16), w_ssm_out.astype(BF16))
    h = _residual_proj(merged, w_out.astype(BF16), h)
    return _mlp(h, mlp_norm_w, w_up.astype(BF16), w_down.astype(BF16))


def kernel(x, mix_norm_w, w_in, q_norm_w, k_norm_w, conv_w, conv_b, dt_bias, a_log, d_skip, ssm_norm_w,
           w_attn_out, w_ssm_out, w_out, mlp_norm_w, w_up, w_down):
    bsz, seq, d = x.shape
    slopes = jnp.exp2(-8.0 * jnp.arange(1, ATTN_HEADS + 1, dtype=F32) / ATTN_HEADS)
    h = x.reshape(bsz * seq, d)
    for layer in range(w_in.shape[0]):
        h = _layer(h, bsz, seq, mix_norm_w[layer], w_in[layer], q_norm_w[layer], k_norm_w[layer], conv_w[layer],
                   conv_b[layer], dt_bias[layer], a_log[layer], d_skip[layer], ssm_norm_w[layer], w_attn_out[layer],
                   w_ssm_out[layer], w_out[layer], mlp_norm_w[layer], w_up[layer], w_down[layer], slopes)
    return h.reshape(bsz, seq, d)
```

```python
import functools

import jax
import jax.numpy as jnp
from jax import lax
from jax.experimental import pallas as pl
from jax.experimental.pallas import tpu as pltpu

D_MODEL = 2048
ATTN_HEAD_DIM = 128
ATTN_HEADS = D_MODEL // ATTN_HEAD_DIM
ATTN_WIDTH = ATTN_HEADS * ATTN_HEAD_DIM
MOBA_BLOCK = 256
MOBA_TOPK = 3
SSM_INNER = 2 * D_MODEL
SSM_HEAD_DIM = 64
SSM_HEADS = SSM_INNER // SSM_HEAD_DIM
SSM_GROUPS = 8
SSM_STATE = 128
SSM_CONV = 4
SSM_CHUNK = 128
SSM_BC = SSM_GROUPS * SSM_STATE
SSM_CONV_DIM = SSM_INNER + 2 * SSM_BC
GROUP_WIDTH = SSM_INNER // SSM_GROUPS
D_FF = 4 * D_MODEL
EPS = 1e-6
NEG_INF = -1e30
LOG2E = 1.4426950408889634
Q_SCALE = ATTN_HEAD_DIM ** -0.5 * LOG2E
SLOPE_TERMS = 3

LANES = 128
CONV_HALO = 8
VMEM_LIMIT = 56 * 1024 * 1024

F32 = jnp.float32
BF16 = jnp.bfloat16

_NT = (((1,), (1,)), ((), ()))


def _params(*sem):
    return pltpu.CompilerParams(dimension_semantics=sem, vmem_limit_bytes=VMEM_LIMIT)


def _dot(a, b):
    return jnp.dot(a, b, preferred_element_type=F32)


def _dot_nt(a, b):
    return lax.dot_general(a, b, _NT, preferred_element_type=F32)


def _rmsnorm_kernel(x_ref, w_ref, o_ref):
    x = x_ref[...]
    ms = jnp.mean(x * x, axis=-1, keepdims=True)
    o_ref[...] = (x * lax.rsqrt(ms + EPS) * w_ref[...]).astype(o_ref.dtype)


def _rmsnorm(x, w, tm=512):
    t, d = x.shape
    return pl.pallas_call(
        _rmsnorm_kernel,
        grid=(t // tm,),
        in_specs=[pl.BlockSpec((tm, d), lambda i: (i, 0)), pl.BlockSpec((1, d), lambda i: (0, 0))],
        out_specs=pl.BlockSpec((tm, d), lambda i: (i, 0)),
        out_shape=jax.ShapeDtypeStruct((t, d), BF16),
        compiler_params=_params("parallel"),
        name="rmsnorm",
    )(x, w.reshape(1, d))


def _proj_kernel(a_ref, w_ref, o_ref):
    o_ref[...] = _dot(a_ref[...], w_ref[...]).astype(o_ref.dtype)


def _proj_t_kernel(a_ref, wt_ref, o_ref):
    o_ref[...] = _dot_nt(wt_ref[...], a_ref[...]).astype(o_ref.dtype)


def _proj_knorm_kernel(a_ref, w_ref, nw_ref, o_ref):
    acc = _dot(a_ref[...], w_ref[...])
    nw = nw_ref[...]
    for h in range(acc.shape[1] // ATTN_HEAD_DIM):
        slab = acc[:, h * ATTN_HEAD_DIM:(h + 1) * ATTN_HEAD_DIM]
        ms = jnp.mean(slab * slab, axis=-1, keepdims=True)
        o_ref[:, h * ATTN_HEAD_DIM:(h + 1) * ATTN_HEAD_DIM] = (slab * lax.rsqrt(ms + EPS) * nw).astype(o_ref.dtype)


def _proj_qnorm_t_kernel(a_ref, wt_ref, nw_ref, o_ref):
    acc = _dot_nt(wt_ref[...], a_ref[...])
    nw = nw_ref[...]
    for h in range(acc.shape[0] // ATTN_HEAD_DIM):
        slab = acc[h * ATTN_HEAD_DIM:(h + 1) * ATTN_HEAD_DIM, :]
        ms = jnp.mean(slab * slab, axis=0, keepdims=True)
        qn = slab * lax.rsqrt(ms + EPS) * nw
        o_ref[h * ATTN_HEAD_DIM:(h + 1) * ATTN_HEAD_DIM, :] = (qn * Q_SCALE).astype(o_ref.dtype)


def _project(a, w, out_dtype, *, transposed=False, norm_w=None, tm=1024, tn=512, name="proj"):
    t, k = a.shape
    n = w.shape[0] if transposed else w.shape[1]
    tn = min(tn, n)
    a_spec = pl.BlockSpec((tm, k), lambda i, j: (i, 0))
    if transposed:
        w_spec = pl.BlockSpec((tn, k), lambda i, j: (j, 0))
        o_spec = pl.BlockSpec((tn, tm), lambda i, j: (j, i))
        o_shape = (n, t)
    else:
        w_spec = pl.BlockSpec((k, tn), lambda i, j: (0, j))
        o_spec = pl.BlockSpec((tm, tn), lambda i, j: (i, j))
        o_shape = (t, n)
    in_specs = [a_spec, w_spec]
    args = [a, w]
    if norm_w is None:
        body = _proj_t_kernel if transposed else _proj_kernel
    else:
        if transposed:
            body = _proj_qnorm_t_kernel
            in_specs.append(pl.BlockSpec((ATTN_HEAD_DIM, 1), lambda i, j: (0, 0)))
            args.append(norm_w.reshape(ATTN_HEAD_DIM, 1))
        else:
            body = _proj_knorm_kernel
            in_specs.append(pl.BlockSpec((1, ATTN_HEAD_DIM), lambda i, j: (0, 0)))
            args.append(norm_w.reshape(1, ATTN_HEAD_DIM))
    return pl.pallas_call(
        body,
        grid=(t // tm, n // tn),
        in_specs=in_specs,
        out_specs=o_spec,
        out_shape=jax.ShapeDtypeStruct(o_shape, out_dtype),
        compiler_params=_params("parallel", "parallel"),
        name=name,
    )(*args)


def _dt_kernel(a_ref, w_ref, wt_ref, o_ref, ot_ref):
    a = a_ref[...]
    o_ref[...] = _dot(a, w_ref[...])
    ot_ref[...] = _dot_nt(wt_ref[...], a)


def _project_dt(a, w, wt, tm=1024):
    t, k = a.shape
    n = w.shape[1]
    return pl.pallas_call(
        _dt_kernel,
        grid=(t // tm,),
        in_specs=[pl.BlockSpec((tm, k), lambda i: (i, 0)),
                  pl.BlockSpec((k, n), lambda i: (0, 0)),
                  pl.BlockSpec((n, k), lambda i: (0, 0))],
        out_specs=[pl.BlockSpec((tm, n), lambda i: (i, 0)), pl.BlockSpec((n, tm), lambda i: (0, i))],
        out_shape=[jax.ShapeDtypeStruct((t, n), F32), jax.ShapeDtypeStruct((n, t), F32)],
        compiler_params=_params("parallel"),
        name="proj_dt",
    )(a, w, wt)


def _moba_kernel(slopes_ref, qt_ref, k_ref, vt_ref, o_ref, kaug_ref, qaug_ref, kmh_ref, kml_ref, sel_ref, acc_ref,
                 sd_ref, s0_ref, s1_ref):
    hg = pl.program_id(1)
    i = pl.program_id(2)
    blk = MOBA_BLOCK
    dh = ATTN_HEAD_DIM
    seq = k_ref.shape[0]
    nb = seq // blk
    n_chain = qt_ref.shape[0] // dh
    n_split = SLOPE_TERMS

    @pl.when(i == 0)
    def _():
        r = lax.broadcasted_iota(jnp.int32, (nb, seq), 0)
        c = lax.broadcasted_iota(jnp.int32, (nb, seq), 1)
        lo = r * blk
        pool = jnp.where((c >= lo) & (c < lo + blk), 1.0 / blk, 0.0).astype(BF16)
        k_lane = lax.broadcasted_iota(jnp.int32, (blk, LANES), 1)
        k_row = lax.broadcasted_iota(jnp.int32, (blk, LANES), 0).astype(F32)
        q_row = lax.broadcasted_iota(jnp.int32, (LANES, blk), 0)
        q_lane = lax.broadcasted_iota(jnp.int32, (LANES, blk), 1).astype(F32)
        for g in range(n_chain):
            kg = k_ref[:, g * dh:(g + 1) * dh]
            kmean = _dot(pool, kg)
            hi = kmean.astype(BF16)
            kmh_ref[g] = hi
            kml_ref[g] = (kmean - hi.astype(F32)).astype(BF16)
            kaug_ref[g, :, 0:dh] = kg
            k_aug = jnp.where((k_lane >= n_split) & (k_lane < 2 * n_split), k_row, 0.0)
            q_aug = jnp.where(q_row < n_split, -q_lane, 0.0)
            for t in range(n_split):
                s_t = slopes_ref[(hg * n_chain + g) * n_split + t]
                k_aug = jnp.where(k_lane == t, s_t, k_aug)
                q_aug = jnp.where(q_row == n_split + t, s_t, q_aug)
            k_aug = k_aug.astype(BF16)
            for jb in range(nb):
                kaug_ref[g, jb * blk:(jb + 1) * blk, dh:dh + LANES] = k_aug
            qaug_ref[g, dh:dh + LANES, :] = q_aug.astype(BF16)

    chains = range(n_chain)

    def score_stage(j, buf_ref):
        jc = jnp.minimum(j, i)
        for g in chains:
            kb = kaug_ref[g, pl.ds(pl.multiple_of(jc * blk, blk), blk), :]
            buf_ref[g] = _dot(kb, qaug_ref[g])

    def values(g, j):
        jc = jnp.minimum(j, i)
        return vt_ref[g * dh:(g + 1) * dh, pl.ds(pl.multiple_of(jc * blk, blk), blk)]

    def softmax_stage(j, buf_ref, ms, ls):
        far = ((i - j) * blk).astype(F32)
        ms_new, ls_new = [], []
        for g in chains:
            m, l = ms[g], ls[g]
            c_shift = slopes_ref[(ATTN_HEADS + hg * n_chain + g) * n_split] * far
            keep = sel_ref[g, pl.ds(jnp.minimum(j, nb - 1), 1), :] > 0.0
            m_new = jnp.maximum(m, jnp.where(keep, jnp.max(buf_ref[g], axis=0, keepdims=True) - c_shift, NEG_INF))
            alpha = jnp.exp2(m - m_new)
            p = jnp.exp2(buf_ref[g] - jnp.where(keep, m_new + c_shift, -NEG_INF))
            ms_new.append(m_new)
            ls_new.append(alpha * l + jnp.sum(p, axis=0, keepdims=True))
            acc_ref[g] = alpha * acc_ref[g] + _dot(values(g, j), p.astype(BF16))
        return tuple(ms_new), tuple(ls_new)

    for g in chains:
        qaug_ref[g, 0:dh, :] = qt_ref[g * dh:(g + 1) * dh, :]
    score_stage(i, sd_ref)
    score_stage(0, s0_ref)

    jidx = lax.broadcasted_iota(jnp.int32, (nb, blk), 0)
    for g in chains:
        qt = qt_ref[g * dh:(g + 1) * dh, :]
        gate = _dot(kmh_ref[g], qt) + _dot(kml_ref[g], qt)
        cnt = jnp.zeros((nb, blk), F32)
        for jp in range(nb):
            row = gate[jp:jp + 1, :]
            beats = (row > gate) | ((row == gate) & (jidx > jp))
            cnt = cnt + jnp.where(beats, (jp < i).astype(F32), 0.0)
        sel_ref[g] = jnp.where((cnt < MOBA_TOPK) & (jidx < i), 1.0, 0.0)

    kl = lax.broadcasted_iota(jnp.int32, (blk, blk), 0)
    ql = lax.broadcasted_iota(jnp.int32, (blk, blk), 1)
    ms, ls = [], []
    for g in chains:
        s = jnp.where(kl <= ql, sd_ref[g], NEG_INF)
        m0 = jnp.max(s, axis=0, keepdims=True)
        p = jnp.exp2(s - m0)
        ms.append(m0)
        ls.append(jnp.sum(p, axis=0, keepdims=True))
        acc_ref[g] = _dot(values(g, i), p.astype(BF16))

    def body(t, carry):
        ms, ls = carry
        j = 2 * t
        score_stage(j + 1, s1_ref)
        ms, ls = softmax_stage(j, s0_ref, ms, ls)
        score_stage(j + 2, s0_ref)
        return softmax_stage(j + 1, s1_ref, ms, ls)

    _, ls = lax.fori_loop(0, (i + 1) // 2, body, (tuple(ms), tuple(ls)))
    for g in chains:
        o_ref[:, g * dh:(g + 1) * dh] = (acc_ref[g] / ls[g]).T.astype(o_ref.dtype)


def _moba_attention(qt, k, vt, slope_table, bsz, seq, heads_per_step=2):
    assert seq % MOBA_BLOCK == 0 and ATTN_HEADS % heads_per_step == 0
    nq = seq // MOBA_BLOCK
    nb = nq
    dh = ATTN_HEAD_DIM
    g = heads_per_step
    return pl.pallas_call(
        _moba_kernel,
        grid=(bsz, ATTN_HEADS // g, nq),
        in_specs=[pl.BlockSpec(memory_space=pltpu.SMEM),
                  pl.BlockSpec((g * dh, MOBA_BLOCK), lambda b, h, i: (h, b * nq + i)),
                  pl.BlockSpec((seq, g * dh), lambda b, h, i: (b, h)),
                  pl.BlockSpec((g * dh, seq), lambda b, h, i: (h, b))],
        out_specs=pl.BlockSpec((MOBA_BLOCK, g * dh), lambda b, h, i: (b * nq + i, h)),
        out_shape=jax.ShapeDtypeStruct((bsz * seq, ATTN_WIDTH), BF16),
        scratch_shapes=[pltpu.VMEM((g, seq, dh + LANES), BF16),
                        pltpu.VMEM((g, dh + LANES, MOBA_BLOCK), BF16),
                        pltpu.VMEM((g, nb, dh), BF16), pltpu.VMEM((g, nb, dh), BF16),
                        pltpu.VMEM((g, nb, MOBA_BLOCK), F32),
                        pltpu.VMEM((g, dh, MOBA_BLOCK), F32)]
                       + [pltpu.VMEM((g, MOBA_BLOCK, MOBA_BLOCK), F32)] * 3,
        compiler_params=_params("parallel", "parallel", "arbitrary"),
        name="moba_attention",
    )(slope_table, qt, k, vt)


def _slope_table():
    slopes = jnp.exp2(-8.0 * jnp.arange(1, ATTN_HEADS + 1, dtype=F32) / ATTN_HEADS) * LOG2E
    terms, rest = [], slopes
    for _ in range(SLOPE_TERMS):
        t = rest.astype(BF16).astype(F32)
        terms.append(t)
        rest = rest - t
    split = jnp.stack(terms, axis=1).reshape(-1)
    full = jnp.stack([slopes] * SLOPE_TERMS, axis=1).reshape(-1)
    return jnp.concatenate([split, full])


def _split3(x):
    h1 = x.astype(BF16)
    r1 = x - h1.astype(F32)
    h2 = r1.astype(BF16)
    h3 = (r1 - h2.astype(F32)).astype(BF16)
    return h1, h2, h3


def _softplus(x):
    return jnp.maximum(x, 0.0) + jnp.log1p(jnp.exp(-jnp.abs(x)))


def _silu(x):
    return x * jax.nn.sigmoid(x)


def _ssd_kernel(xbc_ref, z_ref, dt_ref, dtt_ref, convw_ref, convb_ref, dtb_ref, dtbt_ref, alog_ref, alogt_ref,
                dskip_ref, normw_ref, o_ref, ext_ref, act_ref, state_ref, y_ref):
    c = pl.program_id(1)
    ln = SSM_CHUNK
    halo = CONV_HALO

    @pl.when(c == 0)
    def _():
        ext_ref[0:halo, :] = jnp.zeros((halo, SSM_CONV_DIM), F32)
        state_ref[...] = jnp.zeros_like(state_ref)

    ext_ref[halo:halo + ln, :] = xbc_ref[...]
    cw = 512
    for c0 in range(0, SSM_CONV_DIM, cw):
        acc = jnp.broadcast_to(convb_ref[:, c0:c0 + cw], (ln, cw))
        for kk in range(SSM_CONV):
            r0 = halo - (SSM_CONV - 1) + kk
            acc = acc + convw_ref[kk:kk + 1, c0:c0 + cw] * ext_ref[r0:r0 + ln, c0:c0 + cw]
        act_ref[:, c0:c0 + cw] = _silu(acc)
    ext_ref[0:halo, :] = ext_ref[ln:ln + halo, :]

    dt = _softplus(dt_ref[...] + dtb_ref[...])
    dtt = _softplus(dtt_ref[...] + dtbt_ref[...])
    la = dt * (-jnp.exp(alog_ref[...]))
    lat = dtt * (-jnp.exp(alogt_ref[...]))
    row = lax.broadcasted_iota(jnp.int32, (ln, ln), 0)
    col = lax.broadcasted_iota(jnp.int32, (ln, ln), 1)
    causal = row >= col
    tri = jnp.where(causal, 1.0, 0.0).astype(BF16)
    trit = jnp.where(row <= col, 1.0, 0.0).astype(BF16)
    a1, a2, a3 = _split3(la)
    acum = _dot(tri, a1) + _dot(tri, a2) + _dot(tri, a3)
    b1, b2, b3 = _split3(lat)
    acumt = _dot(b1, trit) + _dot(b2, trit) + _dot(b3, trit)

    left = lax.broadcasted_iota(jnp.int32, (1, LANES), 1) < SSM_HEAD_DIM
    pairs_per_group = GROUP_WIDTH // LANES

    for g in range(SSM_GROUPS):
        b0 = SSM_INNER + g * SSM_STATE
        c0 = SSM_INNER + SSM_BC + g * SSM_STATE
        bg = act_ref[:, b0:b0 + SSM_STATE]
        cgb = act_ref[:, c0:c0 + SSM_STATE].astype(BF16)
        cb = _dot_nt(cgb, bg.astype(BF16))
        bgt = bg.T
        for pp in range(pairs_per_group):
            p = g * pairs_per_group + pp
            x2 = act_ref[:, p * LANES:(p + 1) * LANES].astype(BF16)
            lhs = []
            eacs = []
            for hh in (2 * p, 2 * p + 1):
                colb = jnp.broadcast_to(acum[:, hh:hh + 1], (ln, ln))
                rowa = acumt[hh:hh + 1, :]
                dtr = dtt[hh:hh + 1, :]
                decay = jnp.exp(jnp.where(causal, colb - rowa, -jnp.inf))
                lhs.append((cb * decay * dtr).astype(BF16))
                wt = dtr * jnp.exp(acumt[hh:hh + 1, ln - 1:ln] - rowa)
                lhs.append((bgt * wt).astype(BF16))
                eacs.append(jnp.exp(colb))
            res = _dot(jnp.concatenate(lhs, axis=0), x2)
            res = jnp.where(left, res[0:2 * ln], res[2 * ln:4 * ln])
            eac = jnp.where(left, eacs[0], eacs[1])
            st = state_ref[p]
            y = res[0:ln] + _dot(cgb, st.astype(BF16)) * eac
            y_ref[:, p * LANES:(p + 1) * LANES] = y
            state_ref[p] = st * eac[ln - 1:ln, :] + res[ln:2 * ln]
        g0 = g * GROUP_WIDTH
        yg = y_ref[:, g0:g0 + GROUP_WIDTH] + act_ref[:, g0:g0 + GROUP_WIDTH] * dskip_ref[:, g0:g0 + GROUP_WIDTH]
        yg = yg * _silu(z_ref[:, g0:g0 + GROUP_WIDTH])
        ms = jnp.mean(yg * yg, axis=-1, keepdims=True)
        o_ref[:, g0:g0 + GROUP_WIDTH] = (yg * lax.rsqrt(ms + EPS) * normw_ref[:, g0:g0 + GROUP_WIDTH]).astype(o_ref.dtype)


def _ssd_branch(xbc, z, dt, dtt, conv_w, conv_b, dt_bias, a_log, d_skip, norm_w, bsz, seq):
    assert seq % SSM_CHUNK == 0
    nc = seq // SSM_CHUNK
    ln = SSM_CHUNK
    hs = SSM_HEADS
    row = lambda b, c: (b * nc + c, 0)
    fixed = lambda b, c: (0, 0)
    return pl.pallas_call(
        _ssd_kernel,
        grid=(bsz, nc),
        in_specs=[pl.BlockSpec((ln, SSM_CONV_DIM), row),
                  pl.BlockSpec((ln, SSM_INNER), row),
                  pl.BlockSpec((ln, hs), row),
                  pl.BlockSpec((hs, ln), lambda b, c: (0, b * nc + c)),
                  pl.BlockSpec((SSM_CONV, SSM_CONV_DIM), fixed),
                  pl.BlockSpec((1, SSM_CONV_DIM), fixed),
                  pl.BlockSpec((1, hs), fixed),
                  pl.BlockSpec((hs, 1), fixed),
                  pl.BlockSpec((1, hs), fixed),
                  pl.BlockSpec((hs, 1), fixed),
                  pl.BlockSpec((1, SSM_INNER), fixed),
                  pl.BlockSpec((1, SSM_INNER), fixed)],
        out_specs=pl.BlockSpec((ln, SSM_INNER), row),
        out_shape=jax.ShapeDtypeStruct((bsz * seq, SSM_INNER), BF16),
        scratch_shapes=[pltpu.VMEM((ln + CONV_HALO, SSM_CONV_DIM), F32),
                        pltpu.VMEM((ln, SSM_CONV_DIM), F32),
                        pltpu.VMEM((SSM_HEADS // 2, SSM_STATE, LANES), F32),
                        pltpu.VMEM((ln, SSM_INNER), F32)],
        compiler_params=_params("arbitrary", "arbitrary"),
        name="ssd",
    )(xbc, z, dt, dtt, conv_w, conv_b.reshape(1, -1), dt_bias.reshape(1, hs), dt_bias.reshape(hs, 1),
      a_log.reshape(1, hs), a_log.reshape(hs, 1),
      jnp.repeat(d_skip, SSM_HEAD_DIM).reshape(1, SSM_INNER), norm_w.reshape(1, SSM_INNER))


def _merge_kernel(attn_ref, ssm_ref, ga_ref, gs_ref, wa_ref, ws_ref, o_ref):
    ya = _dot(attn_ref[...], wa_ref[...])
    ys = _dot(ssm_ref[...], ws_ref[...])
    o_ref[...] = (jax.nn.sigmoid(ga_ref[...]) * ya + jax.nn.sigmoid(gs_ref[...]) * ys).astype(o_ref.dtype)


def _merge(attn, ssm, gate, wa, ws, tm=1024, tn=512):
    t = attn.shape[0]
    n = wa.shape[1]
    nj = n // tn
    return pl.pallas_call(
        _merge_kernel,
        grid=(t // tm, nj),
        in_specs=[pl.BlockSpec((tm, attn.shape[1]), lambda i, j: (i, 0)),
                  pl.BlockSpec((tm, ssm.shape[1]), lambda i, j: (i, 0)),
                  pl.BlockSpec((tm, tn), lambda i, j: (i, j)),
                  pl.BlockSpec((tm, tn), lambda i, j: (i, j + nj)),
                  pl.BlockSpec((wa.shape[0], tn), lambda i, j: (0, j)),
                  pl.BlockSpec((ws.shape[0], tn), lambda i, j: (0, j))],
        out_specs=pl.BlockSpec((tm, tn), lambda i, j: (i, j)),
        out_shape=jax.ShapeDtypeStruct((t, n), BF16),
        compiler_params=_params("parallel", "parallel"),
        name="merge",
    )(attn, ssm, gate, gate, wa, ws)


def _residual_proj_kernel(a_ref, w_ref, r_ref, o_ref):
    o_ref[...] = r_ref[...] + _dot(a_ref[...], w_ref[...])


def _residual_proj(a, w, resid, tm=1024, tn=512):
    t, k = a.shape
    n = w.shape[1]
    return pl.pallas_call(
        _residual_proj_kernel,
        grid=(t // tm, n // tn),
        in_specs=[pl.BlockSpec((tm, k), lambda i, j: (i, 0)),
                  pl.BlockSpec((k, tn), lambda i, j: (0, j)),
                  pl.BlockSpec((tm, tn), lambda i, j: (i, j))],
        out_specs=pl.BlockSpec((tm, tn), lambda i, j: (i, j)),
        out_shape=jax.ShapeDtypeStruct((t, n), F32),
        compiler_params=_params("parallel", "parallel"),
        name="out_proj",
    )(a, w, resid)


def _mlp_kernel(h_ref, nw_ref, wu_ref, wd_ref, o_ref, hn_ref):
    j = pl.program_id(1)

    @pl.when(j == 0)
    def _():
        h = h_ref[...]
        ms = jnp.mean(h * h, axis=-1, keepdims=True)
        hn_ref[...] = (h * lax.rsqrt(ms + EPS) * nw_ref[...]).astype(hn_ref.dtype)
        o_ref[...] = h

    u = _dot(hn_ref[...], wu_ref[...])
    act = jnp.square(jnp.maximum(u, 0.0)).astype(BF16)
    o_ref[...] += _dot(act, wd_ref[...])


def _mlp(h, norm_w, w_up, w_down, tm=512, tf=512):
    t, d = h.shape
    f = w_up.shape[1]
    return pl.pallas_call(
        _mlp_kernel,
        grid=(t // tm, f // tf),
        in_specs=[pl.BlockSpec((tm, d), lambda i, j: (i, 0)),
                  pl.BlockSpec((1, d), lambda i, j: (0, 0)),
                  pl.BlockSpec((d, tf), lambda i, j: (0, j)),
                  pl.BlockSpec((tf, d), lambda i, j: (j, 0))],
        out_specs=pl.BlockSpec((tm, d), lambda i, j: (i, 0)),
        out_shape=jax.ShapeDtypeStruct((t, d), F32),
        scratch_shapes=[pltpu.VMEM((tm, d), BF16)],
        compiler_params=_params("parallel", "arbitrary"),
        name="mlp",
    )(h, norm_w.reshape(1, d), w_up, w_down)


def _layer(h, bsz, seq, mix_norm_w, w_in, q_norm_w, k_norm_w, conv_w, conv_b, dt_bias, a_log, d_skip,
           ssm_norm_w, w_attn_out, w_ssm_out, w_out, mlp_norm_w, w_up, w_down, slopes):
    o_k = ATTN_WIDTH
    o_v = 2 * ATTN_WIDTH
    o_z = 3 * ATTN_WIDTH
    o_x = o_z + SSM_INNER
    o_dt = o_x + SSM_CONV_DIM
    o_g = o_dt + SSM_HEADS
    wq_t = w_in[:, :o_k].T.astype(BF16)
    wk = w_in[:, o_k:o_v].astype(BF16)
    wv_t = w_in[:, o_v:o_z].T.astype(BF16)
    wz = w_in[:, o_z:o_x].astype(BF16)
    wx = w_in[:, o_x:o_dt].astype(BF16)
    wdt = w_in[:, o_dt:o_g].astype(BF16)
    wg = w_in[:, o_g:].astype(BF16)

    hn = _rmsnorm(h, mix_norm_w)
    qt = _project(hn, wq_t, BF16, transposed=True, norm_w=q_norm_w, name="proj_q")
    k = _project(hn, wk, BF16, norm_w=k_norm_w, name="proj_k")
    vt = _project(hn, wv_t, BF16, transposed=True, name="proj_v")
    z = _project(hn, wz, F32, name="proj_z")
    xbc = _project(hn, wx, F32, name="proj_xbc")
    gate = _project(hn, wg, F32, name="proj_gate")
    dt, dtt = _project_dt(hn, wdt, wdt.T)

    attn = _moba_attention(qt, k, vt, slopes, bsz, seq)
    ssm = _ssd_branch(xbc, z, dt, dtt, conv_w, conv_b, dt_bias, a_log, d_skip, ssm_norm_w, bsz, seq)
    merged = _merge(attn, ssm, gate, w_attn_out.astype(BF16), w_ssm_out.astype(BF16))
    h = _residual_proj(merged, w_out.astype(BF16), h)
    return _mlp(h, mlp_norm_w, w_up.astype(BF16), w_down.astype(BF16))


def kernel(x, mix_norm_w, w_in, q_norm_w, k_norm_w, conv_w, conv_b, dt_bias, a_log, d_skip, ssm_norm_w,
           w_attn_out, w_ssm_out, w_out, mlp_norm_w, w_up, w_down):
    bsz, seq, d = x.shape
    slopes = _slope_table()
    h = x.reshape(bsz * seq, d)
    for layer in range(w_in.shape[0]):
        h = _layer(h, bsz, seq, mix_norm_w[layer], w_in[layer], q_norm_w[layer], k_norm_w[layer], conv_w[layer],
                   conv_b[layer], dt_bias[layer], a_log[layer], d_skip[layer], ssm_norm_w[layer], w_attn_out[layer],
                   w_ssm_out[layer], w_out[layer], mlp_norm_w[layer], w_up[layer], w_down[layer], slopes)
    return h.reshape(bsz, seq, d)
```

```python
import functools

import jax
import jax.numpy as jnp
from jax import lax
from jax.experimental import pallas as pl
from jax.experimental.pallas import tpu as pltpu

D_MODEL = 2048
ATTN_HEAD_DIM = 128
ATTN_HEADS = D_MODEL // ATTN_HEAD_DIM
ATTN_WIDTH = ATTN_HEADS * ATTN_HEAD_DIM
MOBA_BLOCK = 256
MOBA_TOPK = 3
SSM_INNER = 2 * D_MODEL
SSM_HEAD_DIM = 64
SSM_HEADS = SSM_INNER // SSM_HEAD_DIM
SSM_GROUPS = 8
SSM_STATE = 128
SSM_CONV = 4
SSM_CHUNK = 128
SSM_BC = SSM_GROUPS * SSM_STATE
SSM_CONV_DIM = SSM_INNER + 2 * SSM_BC
GROUP_WIDTH = SSM_INNER // SSM_GROUPS
D_FF = 4 * D_MODEL
EPS = 1e-6
NEG_INF = -1e30
LOG2E = 1.4426950408889634
Q_SCALE = ATTN_HEAD_DIM ** -0.5 * LOG2E
SLOPE_TERMS = 3

LANES = 128
CONV_HALO = 8
PROJ_TM = 1024
PROJ_TN = 1024
CONV_SUB = 256
N_GATES = 2 * D_MODEL
VMEM_LIMIT = 56 * 1024 * 1024

F32 = jnp.float32
BF16 = jnp.bfloat16

_NT = (((1,), (1,)), ((), ()))


def _params(*sem):
    return pltpu.CompilerParams(dimension_semantics=sem, vmem_limit_bytes=VMEM_LIMIT)


def _dot(a, b):
    return jnp.dot(a, b, preferred_element_type=F32)


def _dot_nt(a, b):
    return lax.dot_general(a, b, _NT, preferred_element_type=F32)


def _rmsnorm_kernel(x_ref, w_ref, o_ref):
    x = x_ref[...]
    ms = jnp.mean(x * x, axis=-1, keepdims=True)
    o_ref[...] = (x * lax.rsqrt(ms + EPS) * w_ref[...]).astype(o_ref.dtype)


def _rmsnorm(x, w, tm=512):
    t, d = x.shape
    return pl.pallas_call(
        _rmsnorm_kernel,
        grid=(t // tm,),
        in_specs=[pl.BlockSpec((tm, d), lambda i: (i, 0)), pl.BlockSpec((1, d), lambda i: (0, 0))],
        out_specs=pl.BlockSpec((tm, d), lambda i: (i, 0)),
        out_shape=jax.ShapeDtypeStruct((t, d), BF16),
        compiler_params=_params("parallel"),
        name="rmsnorm",
    )(x, w.reshape(1, d))


def _stage_weight(w_ref, wbf_ref):
    @pl.when(pl.program_id(1) == 0)
    def _():
        wbf_ref[...] = w_ref[...].astype(BF16)


def _silu(x):
    half = 0.5 * x
    return half + half * jnp.tanh(half)


def _proj_act_kernel(a_ref, w_ref, o_ref, wbf_ref, *, act):
    _stage_weight(w_ref, wbf_ref)
    o_ref[...] = act(_dot(a_ref[...], wbf_ref[...])).astype(o_ref.dtype)


def _proj_t_kernel(a_ref, wt_ref, o_ref, wbf_ref):
    _stage_weight(wt_ref, wbf_ref)
    o_ref[...] = _dot_nt(wbf_ref[...], a_ref[...]).astype(o_ref.dtype)


def _proj_knorm_kernel(a_ref, w_ref, nw_ref, o_ref, wbf_ref):
    _stage_weight(w_ref, wbf_ref)
    acc = _dot(a_ref[...], wbf_ref[...])
    nw = nw_ref[...]
    for h in range(acc.shape[1] // ATTN_HEAD_DIM):
        slab = acc[:, h * ATTN_HEAD_DIM:(h + 1) * ATTN_HEAD_DIM]
        ms = jnp.mean(slab * slab, axis=-1, keepdims=True)
        o_ref[:, h * ATTN_HEAD_DIM:(h + 1) * ATTN_HEAD_DIM] = (slab * lax.rsqrt(ms + EPS) * nw).astype(o_ref.dtype)


def _proj_qnorm_t_kernel(a_ref, wt_ref, nw_ref, o_ref, wbf_ref):
    _stage_weight(wt_ref, wbf_ref)
    acc = _dot_nt(wbf_ref[...], a_ref[...])
    nw = nw_ref[...]
    for h in range(acc.shape[0] // ATTN_HEAD_DIM):
        slab = acc[h * ATTN_HEAD_DIM:(h + 1) * ATTN_HEAD_DIM, :]
        ms = jnp.mean(slab * slab, axis=0, keepdims=True)
        qn = slab * lax.rsqrt(ms + EPS) * nw
        o_ref[h * ATTN_HEAD_DIM:(h + 1) * ATTN_HEAD_DIM, :] = (qn * Q_SCALE).astype(o_ref.dtype)


def _proj_conv_kernel(a_ref, w_ref, cw_ref, cb_ref, o_ref, wbf_ref, tail_ref, *, seq, sub):
    _stage_weight(w_ref, wbf_ref)
    i = pl.program_id(1)
    tm = a_ref.shape[0]
    w = wbf_ref[...]

    @pl.when((i * tm) % seq == 0)
    def _():
        tail_ref[...] = jnp.zeros_like(tail_ref)

    tail = tail_ref[...]
    for r in range(0, tm, sub):
        cur = _dot(a_ref[r:r + sub, :], w)
        ext = jnp.concatenate([tail, cur], axis=0)
        acc = cb_ref[...] + cw_ref[SSM_CONV - 1:SSM_CONV, :] * cur
        for back in range(1, SSM_CONV):
            shifted = pltpu.roll(ext, back, axis=0)[CONV_HALO:CONV_HALO + sub, :]
            acc = acc + cw_ref[SSM_CONV - 1 - back:SSM_CONV - back, :] * shifted
        o_ref[r:r + sub, :] = _silu(acc).astype(o_ref.dtype)
        tail = cur[sub - CONV_HALO:sub, :]
    tail_ref[...] = tail


def _project(a, w, body, *, n, col0=0, transposed=False, extra=(), extra_specs=(), scratch=(),
             out_dtype=BF16, tm=PROJ_TM, tn=PROJ_TN, name="proj"):
    t, k = a.shape
    assert n % tn == 0 and col0 % tn == 0 and t % tm == 0
    c0 = col0 // tn
    in_specs = [pl.BlockSpec((tm, k), lambda j, i: (i, 0))]
    args = [a]
    if transposed:
        in_specs.append(pl.BlockSpec((tn, k), lambda j, i: (j, 0)))
        o_spec = pl.BlockSpec((tn, tm), lambda j, i: (j, i))
        o_shape = (n, t)
        w_block = (tn, k)
    else:
        in_specs.append(pl.BlockSpec((k, tn), lambda j, i: (0, c0 + j)))
        o_spec = pl.BlockSpec((tm, tn), lambda j, i: (i, j))
        o_shape = (t, n)
        w_block = (k, tn)
    args.append(w)
    return pl.pallas_call(
        body,
        grid=(n // tn, t // tm),
        in_specs=in_specs + list(extra_specs),
        out_specs=o_spec,
        out_shape=jax.ShapeDtypeStruct(o_shape, out_dtype),
        scratch_shapes=[pltpu.VMEM(w_block, BF16)] + list(scratch),
        compiler_params=_params("parallel", "arbitrary"),
        name=name,
    )(*args, *extra)


def _dt_kernel(a_ref, w_ref, wt_ref, o_ref, ot_ref):
    a = a_ref[...]
    o_ref[...] = _dot(a, w_ref[...])
    ot_ref[...] = _dot_nt(wt_ref[...], a)


def _project_dt(a, w, wt, tm=1024):
    t, k = a.shape
    n = w.shape[1]
    return pl.pallas_call(
        _dt_kernel,
        grid=(t // tm,),
        in_specs=[pl.BlockSpec((tm, k), lambda i: (i, 0)),
                  pl.BlockSpec((k, n), lambda i: (0, 0)),
                  pl.BlockSpec((n, k), lambda i: (0, 0))],
        out_specs=[pl.BlockSpec((tm, n), lambda i: (i, 0)), pl.BlockSpec((n, tm), lambda i: (0, i))],
        out_shape=[jax.ShapeDtypeStruct((t, n), F32), jax.ShapeDtypeStruct((n, t), F32)],
        compiler_params=_params("parallel"),
        name="proj_dt",
    )(a, w, wt)


def _moba_kernel(slopes_ref, qt_ref, k_ref, vt_ref, o_ref, kaug_ref, qaug_ref, kmh_ref, kml_ref, sel_ref, acc_ref,
                 sd_ref, s0_ref, s1_ref):
    hg = pl.program_id(1)
    i = pl.program_id(2)
    blk = MOBA_BLOCK
    dh = ATTN_HEAD_DIM
    seq = k_ref.shape[0]
    nb = seq // blk
    n_chain = qt_ref.shape[0] // dh
    n_split = SLOPE_TERMS

    @pl.when(i == 0)
    def _():
        r = lax.broadcasted_iota(jnp.int32, (nb, seq), 0)
        c = lax.broadcasted_iota(jnp.int32, (nb, seq), 1)
        lo = r * blk
        pool = jnp.where((c >= lo) & (c < lo + blk), 1.0 / blk, 0.0).astype(BF16)
        k_lane = lax.broadcasted_iota(jnp.int32, (blk, LANES), 1)
        k_row = lax.broadcasted_iota(jnp.int32, (blk, LANES), 0).astype(F32)
        q_row = lax.broadcasted_iota(jnp.int32, (LANES, blk), 0)
        q_lane = lax.broadcasted_iota(jnp.int32, (LANES, blk), 1).astype(F32)
        for g in range(n_chain):
            kg = k_ref[:, g * dh:(g + 1) * dh]
            kmean = _dot(pool, kg)
            hi = kmean.astype(BF16)
            kmh_ref[g] = hi
            kml_ref[g] = (kmean - hi.astype(F32)).astype(BF16)
            kaug_ref[g, :, 0:dh] = kg
            k_aug = jnp.where((k_lane >= n_split) & (k_lane < 2 * n_split), k_row, 0.0)
            q_aug = jnp.where(q_row < n_split, -q_lane, 0.0)
            for t in range(n_split):
                s_t = slopes_ref[(hg * n_chain + g) * n_split + t]
                k_aug = jnp.where(k_lane == t, s_t, k_aug)
                q_aug = jnp.where(q_row == n_split + t, s_t, q_aug)
            k_aug = k_aug.astype(BF16)
            for jb in range(nb):
                kaug_ref[g, jb * blk:(jb + 1) * blk, dh:dh + LANES] = k_aug
            qaug_ref[g, dh:dh + LANES, :] = q_aug.astype(BF16)

    chains = range(n_chain)

    def score_stage(j, buf_ref):
        jc = jnp.minimum(j, i)
        for g in chains:
            kb = kaug_ref[g, pl.ds(pl.multiple_of(jc * blk, blk), blk), :]
            buf_ref[g] = _dot(kb, qaug_ref[g])

    def values(g, j):
        jc = jnp.minimum(j, i)
        return vt_ref[g * dh:(g + 1) * dh, pl.ds(pl.multiple_of(jc * blk, blk), blk)]

    def softmax_stage(j, buf_ref, ms, ls):
        far = ((i - j) * blk).astype(F32)
        ms_new, ls_new = [], []
        for g in chains:
            m, l = ms[g], ls[g]
            c_shift = slopes_ref[(ATTN_HEADS + hg * n_chain + g) * n_split] * far
            keep = sel_ref[g, pl.ds(jnp.minimum(j, nb - 1), 1), :] > 0.0
            m_new = jnp.maximum(m, jnp.where(keep, jnp.max(buf_ref[g], axis=0, keepdims=True) - c_shift, NEG_INF))
            alpha = jnp.exp2(m - m_new)
            p = jnp.exp2(buf_ref[g] - jnp.where(keep, m_new + c_shift, -NEG_INF))
            ms_new.append(m_new)
            ls_new.append(alpha * l + jnp.sum(p, axis=0, keepdims=True))
            acc_ref[g] = alpha * acc_ref[g] + _dot(values(g, j), p.astype(BF16))
        return tuple(ms_new), tuple(ls_new)

    for g in chains:
        qaug_ref[g, 0:dh, :] = qt_ref[g * dh:(g + 1) * dh, :]
    score_stage(i, sd_ref)
    score_stage(0, s0_ref)

    jidx = lax.broadcasted_iota(jnp.int32, (nb, blk), 0)
    for g in chains:
        qt = qt_ref[g * dh:(g + 1) * dh, :]
        gate = _dot(kmh_ref[g], qt) + _dot(kml_ref[g], qt)
        cnt = jnp.zeros((nb, blk), F32)
        for jp in range(nb):
            row = gate[jp:jp + 1, :]
            beats = (row > gate) | ((row == gate) & (jidx > jp))
            cnt = cnt + jnp.where(beats, (jp < i).astype(F32), 0.0)
        sel_ref[g] = jnp.where((cnt < MOBA_TOPK) & (jidx < i), 1.0, 0.0)

    kl = lax.broadcasted_iota(jnp.int32, (blk, blk), 0)
    ql = lax.broadcasted_iota(jnp.int32, (blk, blk), 1)
    ms, ls = [], []
    for g in chains:
        s = jnp.where(kl <= ql, sd_ref[g], NEG_INF)
        m0 = jnp.max(s, axis=0, keepdims=True)
        p = jnp.exp2(s - m0)
        ms.append(m0)
        ls.append(jnp.sum(p, axis=0, keepdims=True))
        acc_ref[g] = _dot(values(g, i), p.astype(BF16))

    def body(t, carry):
        ms, ls = carry
        j = 2 * t
        score_stage(j + 1, s1_ref)
        ms, ls = softmax_stage(j, s0_ref, ms, ls)
        score_stage(j + 2, s0_ref)
        return softmax_stage(j + 1, s1_ref, ms, ls)

    _, ls = lax.fori_loop(0, (i + 1) // 2, body, (tuple(ms), tuple(ls)))
    for g in chains:
        o_ref[:, g * dh:(g + 1) * dh] = (acc_ref[g] / ls[g]).T.astype(o_ref.dtype)


def _moba_attention(qt, k, vt, slope_table, bsz, seq, heads_per_step=4):
    assert seq % MOBA_BLOCK == 0 and ATTN_HEADS % heads_per_step == 0
    nq = seq // MOBA_BLOCK
    nb = nq
    dh = ATTN_HEAD_DIM
    g = heads_per_step
    return pl.pallas_call(
        _moba_kernel,
        grid=(bsz, ATTN_HEADS // g, nq),
        in_specs=[pl.BlockSpec(memory_space=pltpu.SMEM),
                  pl.BlockSpec((g * dh, MOBA_BLOCK), lambda b, h, i: (h, b * nq + i)),
                  pl.BlockSpec((seq, g * dh), lambda b, h, i: (b, h)),
                  pl.BlockSpec((g * dh, seq), lambda b, h, i: (h, b))],
        out_specs=pl.BlockSpec((MOBA_BLOCK, g * dh), lambda b, h, i: (b * nq + i, h)),
        out_shape=jax.ShapeDtypeStruct((bsz * seq, ATTN_WIDTH), BF16),
        scratch_shapes=[pltpu.VMEM((g, seq, dh + LANES), BF16),
                        pltpu.VMEM((g, dh + LANES, MOBA_BLOCK), BF16),
                        pltpu.VMEM((g, nb, dh), BF16), pltpu.VMEM((g, nb, dh), BF16),
                        pltpu.VMEM((g, nb, MOBA_BLOCK), F32),
                        pltpu.VMEM((g, dh, MOBA_BLOCK), F32)]
                       + [pltpu.VMEM((g, MOBA_BLOCK, MOBA_BLOCK), F32)] * 3,
        compiler_params=_params("parallel", "parallel", "arbitrary"),
        name="moba_attention",
    )(slope_table, qt, k, vt)


def _slope_table():
    slopes = jnp.exp2(-8.0 * jnp.arange(1, ATTN_HEADS + 1, dtype=F32) / ATTN_HEADS) * LOG2E
    terms, rest = [], slopes
    for _ in range(SLOPE_TERMS):
        t = rest.astype(BF16).astype(F32)
        terms.append(t)
        rest = rest - t
    split = jnp.stack(terms, axis=1).reshape(-1)
    full = jnp.stack([slopes] * SLOPE_TERMS, axis=1).reshape(-1)
    return jnp.concatenate([split, full])


def _split3(x):
    h1 = x.astype(BF16)
    r1 = x - h1.astype(F32)
    h2 = r1.astype(BF16)
    h3 = (r1 - h2.astype(F32)).astype(BF16)
    return h1, h2, h3


def _softplus(x):
    return jnp.maximum(x, 0.0) + jnp.log1p(jnp.exp(-jnp.abs(x)))


def _ssd_kernel(act_ref, z_ref, dt_ref, dtt_ref, dtb_ref, dtbt_ref, alog_ref, alogt_ref,
                dskip_ref, normw_ref, o_ref, state_ref, y_ref):
    c = pl.program_id(1)
    ln = SSM_CHUNK

    @pl.when(c == 0)
    def _():
        state_ref[...] = jnp.zeros_like(state_ref)

    dt = _softplus(dt_ref[...] + dtb_ref[...])
    dtt = _softplus(dtt_ref[...] + dtbt_ref[...])
    la = dt * (-jnp.exp(alog_ref[...]))
    lat = dtt * (-jnp.exp(alogt_ref[...]))
    row = lax.broadcasted_iota(jnp.int32, (ln, ln), 0)
    col = lax.broadcasted_iota(jnp.int32, (ln, ln), 1)
    causal = row >= col
    tri = jnp.where(causal, 1.0, 0.0).astype(BF16)
    trit = jnp.where(row <= col, 1.0, 0.0).astype(BF16)
    a1, a2, a3 = _split3(la)
    acum = _dot(tri, a1) + _dot(tri, a2) + _dot(tri, a3)
    b1, b2, b3 = _split3(lat)
    acumt = _dot(b1, trit) + _dot(b2, trit) + _dot(b3, trit)

    left = lax.broadcasted_iota(jnp.int32, (1, LANES), 1) < SSM_HEAD_DIM
    pairs_per_group = GROUP_WIDTH // LANES

    for g in range(SSM_GROUPS):
        b0 = SSM_INNER + g * SSM_STATE
        c0 = SSM_INNER + SSM_BC + g * SSM_STATE
        bgb = act_ref[:, b0:b0 + SSM_STATE]
        cgb = act_ref[:, c0:c0 + SSM_STATE]
        cb = _dot_nt(cgb, bgb)
        bgt = bgb.astype(F32).T
        for pp in range(pairs_per_group):
            p = g * pairs_per_group + pp
            x2 = act_ref[:, p * LANES:(p + 1) * LANES]
            lhs = []
            eacs = []
            for hh in (2 * p, 2 * p + 1):
                colb = jnp.broadcast_to(acum[:, hh:hh + 1], (ln, ln))
                rowa = acumt[hh:hh + 1, :]
                dtr = dtt[hh:hh + 1, :]
                decay = jnp.exp(jnp.where(causal, colb - rowa, -jnp.inf))
                lhs.append((cb * decay * dtr).astype(BF16))
                wt = dtr * jnp.exp(acumt[hh:hh + 1, ln - 1:ln] - rowa)
                lhs.append((bgt * wt).astype(BF16))
                eacs.append(jnp.exp(colb))
            res = _dot(jnp.concatenate(lhs, axis=0), x2)
            res = jnp.where(left, res[0:2 * ln], res[2 * ln:4 * ln])
            eac = jnp.where(left, eacs[0], eacs[1])
            st = state_ref[p]
            y = res[0:ln] + _dot(cgb, st.astype(BF16)) * eac
            y_ref[:, p * LANES:(p + 1) * LANES] = y
            state_ref[p] = st * eac[ln - 1:ln, :] + res[ln:2 * ln]
        g0 = g * GROUP_WIDTH
        xg = act_ref[:, g0:g0 + GROUP_WIDTH].astype(F32)
        yg = y_ref[:, g0:g0 + GROUP_WIDTH] + xg * dskip_ref[:, g0:g0 + GROUP_WIDTH]
        yg = yg * z_ref[:, g0:g0 + GROUP_WIDTH].astype(F32)
        ms = jnp.mean(yg * yg, axis=-1, keepdims=True)
        o_ref[:, g0:g0 + GROUP_WIDTH] = (yg * lax.rsqrt(ms + EPS) * normw_ref[:, g0:g0 + GROUP_WIDTH]).astype(o_ref.dtype)


def _ssd_branch(act, zact, dt, dtt, dt_bias, a_log, d_skip, norm_w, bsz, seq):
    assert seq % SSM_CHUNK == 0
    nc = seq // SSM_CHUNK
    ln = SSM_CHUNK
    hs = SSM_HEADS
    row = lambda b, c: (b * nc + c, 0)
    fixed = lambda b, c: (0, 0)
    return pl.pallas_call(
        _ssd_kernel,
        grid=(bsz, nc),
        in_specs=[pl.BlockSpec((ln, SSM_CONV_DIM), row),
                  pl.BlockSpec((ln, SSM_INNER), row),
                  pl.BlockSpec((ln, hs), row),
                  pl.BlockSpec((hs, ln), lambda b, c: (0, b * nc + c)),
                  pl.BlockSpec((1, hs), fixed),
                  pl.BlockSpec((hs, 1), fixed),
                  pl.BlockSpec((1, hs), fixed),
                  pl.BlockSpec((hs, 1), fixed),
                  pl.BlockSpec((1, SSM_INNER), fixed),
                  pl.BlockSpec((1, SSM_INNER), fixed)],
        out_specs=pl.BlockSpec((ln, SSM_INNER), row),
        out_shape=jax.ShapeDtypeStruct((bsz * seq, SSM_INNER), BF16),
        scratch_shapes=[pltpu.VMEM((SSM_HEADS // 2, SSM_STATE, LANES), F32),
                        pltpu.VMEM((ln, SSM_INNER), F32)],
        compiler_params=_params("arbitrary", "arbitrary"),
        name="ssd",
    )(act, zact, dt, dtt, dt_bias.reshape(1, hs), dt_bias.reshape(hs, 1),
      a_log.reshape(1, hs), a_log.reshape(hs, 1),
      jnp.repeat(d_skip, SSM_HEAD_DIM).reshape(1, SSM_INNER), norm_w.reshape(1, SSM_INNER))


def _merge_kernel(attn_ref, ssm_ref, ga_ref, gs_ref, wa_ref, ws_ref, o_ref):
    ya = _dot(attn_ref[...], wa_ref[...])
    ys = _dot(ssm_ref[...], ws_ref[...])
    o_ref[...] = (ga_ref[...].astype(F32) * ya + gs_ref[...].astype(F32) * ys).astype(o_ref.dtype)


def _merge(attn, ssm, gate, wa, ws, tm=1024, tn=512):
    t = attn.shape[0]
    n = wa.shape[1]
    nj = n // tn
    return pl.pallas_call(
        _merge_kernel,
        grid=(t // tm, nj),
        in_specs=[pl.BlockSpec((tm, attn.shape[1]), lambda i, j: (i, 0)),
                  pl.BlockSpec((tm, ssm.shape[1]), lambda i, j: (i, 0)),
                  pl.BlockSpec((tm, tn), lambda i, j: (i, j)),
                  pl.BlockSpec((tm, tn), lambda i, j: (i, j + nj)),
                  pl.BlockSpec((wa.shape[0], tn), lambda i, j: (0, j)),
                  pl.BlockSpec((ws.shape[0], tn), lambda i, j: (0, j))],
        out_specs=pl.BlockSpec((tm, tn), lambda i, j: (i, j)),
        out_shape=jax.ShapeDtypeStruct((t, n), BF16),
        compiler_params=_params("parallel", "parallel"),
        name="merge",
    )(attn, ssm, gate, gate, wa, ws)


def _residual_proj_kernel(a_ref, w_ref, r_ref, o_ref):
    o_ref[...] = r_ref[...] + _dot(a_ref[...], w_ref[...])


def _residual_proj(a, w, resid, tm=1024, tn=512):
    t, k = a.shape
    n = w.shape[1]
    return pl.pallas_call(
        _residual_proj_kernel,
        grid=(t // tm, n // tn),
        in_specs=[pl.BlockSpec((tm, k), lambda i, j: (i, 0)),
                  pl.BlockSpec((k, tn), lambda i, j: (0, j)),
                  pl.BlockSpec((tm, tn), lambda i, j: (i, j))],
        out_specs=pl.BlockSpec((tm, tn), lambda i, j: (i, j)),
        out_shape=jax.ShapeDtypeStruct((t, n), F32),
        compiler_params=_params("parallel", "parallel"),
        name="out_proj",
    )(a, w, resid)


def _mlp_kernel(h_ref, nw_ref, wu_ref, wd_ref, o_ref, hn_ref):
    j = pl.program_id(1)

    @pl.when(j == 0)
    def _():
        h = h_ref[...]
        ms = jnp.mean(h * h, axis=-1, keepdims=True)
        hn_ref[...] = (h * lax.rsqrt(ms + EPS) * nw_ref[...]).astype(hn_ref.dtype)
        o_ref[...] = h

    u = _dot(hn_ref[...], wu_ref[...])
    act = jnp.square(jnp.maximum(u, 0.0)).astype(BF16)
    o_ref[...] += _dot(act, wd_ref[...])


def _mlp(h, norm_w, w_up, w_down, tm=512, tf=512):
    t, d = h.shape
    f = w_up.shape[1]
    return pl.pallas_call(
        _mlp_kernel,
        grid=(t // tm, f // tf),
        in_specs=[pl.BlockSpec((tm, d), lambda i, j: (i, 0)),
                  pl.BlockSpec((1, d), lambda i, j: (0, 0)),
                  pl.BlockSpec((d, tf), lambda i, j: (0, j)),
                  pl.BlockSpec((tf, d), lambda i, j: (j, 0))],
        out_specs=pl.BlockSpec((tm, d), lambda i, j: (i, 0)),
        out_shape=jax.ShapeDtypeStruct((t, d), F32),
        scratch_shapes=[pltpu.VMEM((tm, d), BF16)],
        compiler_params=_params("parallel", "arbitrary"),
        name="mlp",
    )(h, norm_w.reshape(1, d), w_up, w_down)


def _layer(h, bsz, seq, mix_norm_w, w_in, q_norm_w, k_norm_w, conv_w, conv_b, dt_bias, a_log, d_skip,
           ssm_norm_w, w_attn_out, w_ssm_out, w_out, mlp_norm_w, w_up, w_down, slopes):
    o_k = ATTN_WIDTH
    o_v = 2 * ATTN_WIDTH
    o_z = 3 * ATTN_WIDTH
    o_x = o_z + SSM_INNER
    o_dt = o_x + SSM_CONV_DIM
    o_g = o_dt + SSM_HEADS
    wq_t = w_in[:, :o_k].T
    wv_t = w_in[:, o_v:o_z].T
    wdt = w_in[:, o_dt:o_g].astype(BF16)
    wg = w_in[:, o_g:]
    head_col = pl.BlockSpec((ATTN_HEAD_DIM, 1), lambda j, i: (0, 0))
    head_row = pl.BlockSpec((1, ATTN_HEAD_DIM), lambda j, i: (0, 0))
    tn = PROJ_TN

    hn = _rmsnorm(h, mix_norm_w)
    qt = _project(hn, wq_t, _proj_qnorm_t_kernel, n=ATTN_WIDTH, transposed=True,
                  extra=(q_norm_w.reshape(ATTN_HEAD_DIM, 1),), extra_specs=(head_col,), name="proj_q")
    k = _project(hn, w_in, _proj_knorm_kernel, n=ATTN_WIDTH, col0=o_k,
                 extra=(k_norm_w.reshape(1, ATTN_HEAD_DIM),), extra_specs=(head_row,), name="proj_k")
    vt = _project(hn, wv_t, _proj_t_kernel, n=ATTN_WIDTH, transposed=True, name="proj_v")
    zact = _project(hn, w_in, functools.partial(_proj_act_kernel, act=_silu), n=SSM_INNER, col0=o_z, name="proj_z")
    act = _project(hn, w_in, functools.partial(_proj_conv_kernel, seq=seq, sub=CONV_SUB), n=SSM_CONV_DIM, col0=o_x,
                   extra=(conv_w, conv_b.reshape(1, -1)),
                   extra_specs=(pl.BlockSpec((SSM_CONV, tn), lambda j, i: (0, j)),
                                pl.BlockSpec((1, tn), lambda j, i: (0, j))),
                   scratch=(pltpu.VMEM((CONV_HALO, tn), F32),), name="proj_xbc")
    gates = _project(hn, wg, functools.partial(_proj_act_kernel, act=jax.nn.sigmoid), n=N_GATES, name="proj_gate")
    dt, dtt = _project_dt(hn, wdt, wdt.T)

    attn = _moba_attention(qt, k, vt, slopes, bsz, seq)
    ssm = _ssd_branch(act, zact, dt, dtt, dt_bias, a_log, d_skip, ssm_norm_w, bsz, seq)
    merged = _merge(attn, ssm, gates, w_attn_out.astype(BF16), w_ssm_out.astype(BF16))
    h = _residual_proj(merged, w_out.astype(BF16), h)
    return _mlp(h, mlp_norm_w, w_up.astype(BF16), w_down.astype(BF16))


def kernel(x, mix_norm_w, w_in, q_norm_w, k_norm_w, conv_w, conv_b, dt_bias, a_log, d_skip, ssm_norm_w,
           w_attn_out, w_ssm_out, w_out, mlp_norm_w, w_up, w_down):
    bsz, seq, d = x.shape
    slopes = _slope_table()
    h = x.reshape(bsz * seq, d)
    for layer in range(w_in.shape[0]):
        h = _layer(h, bsz, seq, mix_norm_w[layer], w_in[layer], q_norm_w[layer], k_norm_w[layer], conv_w[layer],
                   conv_b[layer], dt_bias[layer], a_log[layer], d_skip[layer], ssm_norm_w[layer], w_attn_out[layer],
                   w_ssm_out[layer], w_out[layer], mlp_norm_w[layer], w_up[layer], w_down[layer], slopes)
    return h.reshape(bsz, seq, d)
```

```python
import functools

import jax
import jax.numpy as jnp
from jax import lax
from jax.experimental import pallas as pl
from jax.experimental.pallas import tpu as pltpu

D_MODEL = 2048
ATTN_HEAD_DIM = 128
ATTN_HEADS = D_MODEL // ATTN_HEAD_DIM
ATTN_WIDTH = ATTN_HEADS * ATTN_HEAD_DIM
MOBA_BLOCK = 256
MOBA_TOPK = 3
SSM_INNER = 2 * D_MODEL
SSM_HEAD_DIM = 64
SSM_HEADS = SSM_INNER // SSM_HEAD_DIM
SSM_GROUPS = 8
SSM_STATE = 128
SSM_CONV = 4
SSM_CHUNK = 128
SSM_BC = SSM_GROUPS * SSM_STATE
SSM_CONV_DIM = SSM_INNER + 2 * SSM_BC
GROUP_WIDTH = SSM_INNER // SSM_GROUPS
D_FF = 4 * D_MODEL
EPS = 1e-6
NEG_INF = -1e30
LOG2E = 1.4426950408889634
Q_SCALE = ATTN_HEAD_DIM ** -0.5 * LOG2E
SLOPE_TERMS = 3

LANES = 128
CONV_HALO = 8
PROJ_TM = 1024
PROJ_TN = 1024
CONV_SUB = 256
N_GATES = 2 * D_MODEL
GATE_SKEW = SSM_HEADS
VMEM_LIMIT = 56 * 1024 * 1024

F32 = jnp.float32
BF16 = jnp.bfloat16

_NT = (((1,), (1,)), ((), ()))


def _params(*sem):
    return pltpu.CompilerParams(dimension_semantics=sem, vmem_limit_bytes=VMEM_LIMIT)


def _dot(a, b):
    return jnp.dot(a, b, preferred_element_type=F32)


def _dot_nt(a, b):
    return lax.dot_general(a, b, _NT, preferred_element_type=F32)


def _rmsnorm_kernel(x_ref, w_ref, o_ref):
    x = x_ref[...]
    ms = jnp.mean(x * x, axis=-1, keepdims=True)
    o_ref[...] = (x * lax.rsqrt(ms + EPS) * w_ref[...]).astype(o_ref.dtype)


def _rmsnorm(x, w, tm=512):
    t, d = x.shape
    return pl.pallas_call(
        _rmsnorm_kernel,
        grid=(t // tm,),
        in_specs=[pl.BlockSpec((tm, d), lambda i: (i, 0)), pl.BlockSpec((1, d), lambda i: (0, 0))],
        out_specs=pl.BlockSpec((tm, d), lambda i: (i, 0)),
        out_shape=jax.ShapeDtypeStruct((t, d), BF16),
        compiler_params=_params("parallel"),
        name="rmsnorm",
    )(x, w.reshape(1, d))


def _stage_weight(w_ref, wbf_ref, transpose=True):
    @pl.when(pl.program_id(1) == 0)
    def _():
        w = w_ref[...]
        wbf_ref[...] = (w.T if transpose else w).astype(BF16)


def _silu(x):
    half = 0.5 * x
    return half + half * jnp.tanh(half)


def _proj_act_kernel(a_ref, w_ref, o_ref, wbf_ref, *, act):
    _stage_weight(w_ref, wbf_ref)
    o_ref[...] = act(_dot(a_ref[...], wbf_ref[...])).astype(o_ref.dtype)


def _proj_gate_kernel(a_ref, w_ref, wnext_ref, o_ref, wbf_ref):
    @pl.when(pl.program_id(1) == 0)
    def _():
        w = jnp.concatenate([w_ref[GATE_SKEW:, :], wnext_ref[...]], axis=0)
        wbf_ref[...] = w.T.astype(BF16)

    o_ref[...] = jax.nn.sigmoid(_dot(a_ref[...], wbf_ref[...])).astype(o_ref.dtype)


def _proj_t_kernel(a_ref, wt_ref, o_ref, wbf_ref):
    _stage_weight(wt_ref, wbf_ref, transpose=False)
    o_ref[...] = _dot_nt(wbf_ref[...], a_ref[...]).astype(o_ref.dtype)


def _proj_knorm_kernel(a_ref, w_ref, nw_ref, o_ref, wbf_ref):
    _stage_weight(w_ref, wbf_ref)
    acc = _dot(a_ref[...], wbf_ref[...])
    nw = nw_ref[...]
    for h in range(acc.shape[1] // ATTN_HEAD_DIM):
        slab = acc[:, h * ATTN_HEAD_DIM:(h + 1) * ATTN_HEAD_DIM]
        ms = jnp.mean(slab * slab, axis=-1, keepdims=True)
        o_ref[:, h * ATTN_HEAD_DIM:(h + 1) * ATTN_HEAD_DIM] = (slab * lax.rsqrt(ms + EPS) * nw).astype(o_ref.dtype)


def _proj_qnorm_t_kernel(a_ref, wt_ref, nw_ref, o_ref, wbf_ref):
    _stage_weight(wt_ref, wbf_ref, transpose=False)
    acc = _dot_nt(wbf_ref[...], a_ref[...])
    nw = nw_ref[...]
    for h in range(acc.shape[0] // ATTN_HEAD_DIM):
        slab = acc[h * ATTN_HEAD_DIM:(h + 1) * ATTN_HEAD_DIM, :]
        ms = jnp.mean(slab * slab, axis=0, keepdims=True)
        qn = slab * lax.rsqrt(ms + EPS) * nw
        o_ref[h * ATTN_HEAD_DIM:(h + 1) * ATTN_HEAD_DIM, :] = (qn * Q_SCALE).astype(o_ref.dtype)


def _proj_conv_kernel(a_ref, w_ref, cw_ref, cb_ref, o_ref, wbf_ref, tail_ref, *, seq, sub):
    _stage_weight(w_ref, wbf_ref)
    i = pl.program_id(1)
    tm = a_ref.shape[0]
    w = wbf_ref[...]

    @pl.when((i * tm) % seq == 0)
    def _():
        tail_ref[...] = jnp.zeros_like(tail_ref)

    tail = tail_ref[...]
    for r in range(0, tm, sub):
        cur = _dot(a_ref[r:r + sub, :], w)
        ext = jnp.concatenate([tail, cur], axis=0)
        acc = cb_ref[...] + cw_ref[SSM_CONV - 1:SSM_CONV, :] * cur
        for back in range(1, SSM_CONV):
            shifted = pltpu.roll(ext, back, axis=0)[CONV_HALO:CONV_HALO + sub, :]
            acc = acc + cw_ref[SSM_CONV - 1 - back:SSM_CONV - back, :] * shifted
        o_ref[r:r + sub, :] = _silu(acc).astype(o_ref.dtype)
        tail = cur[sub - CONV_HALO:sub, :]
    tail_ref[...] = tail


def _project(a, w, body, *, n, col0=0, transposed=False, extra=(), extra_specs=(), scratch=(),
             out_dtype=BF16, tm=PROJ_TM, tn=PROJ_TN, name="proj"):
    t, k = a.shape
    assert n % tn == 0 and col0 % tn == 0 and t % tm == 0
    c0 = col0 // tn
    in_specs = [pl.BlockSpec((tm, k), lambda j, i: (i, 0)), pl.BlockSpec((tn, k), lambda j, i: (c0 + j, 0))]
    args = [a, w]
    if transposed:
        o_spec = pl.BlockSpec((tn, tm), lambda j, i: (j, i))
        o_shape = (n, t)
        w_block = (tn, k)
    else:
        o_spec = pl.BlockSpec((tm, tn), lambda j, i: (i, j))
        o_shape = (t, n)
        w_block = (k, tn)
    return pl.pallas_call(
        body,
        grid=(n // tn, t // tm),
        in_specs=in_specs + list(extra_specs),
        out_specs=o_spec,
        out_shape=jax.ShapeDtypeStruct(o_shape, out_dtype),
        scratch_shapes=[pltpu.VMEM(w_block, BF16)] + list(scratch),
        compiler_params=_params("parallel", "arbitrary"),
        name=name,
    )(*args, *extra)


def _dt_kernel(a_ref, w_ref, wt_ref, o_ref, ot_ref):
    a = a_ref[...]
    o_ref[...] = _dot(a, w_ref[...])
    ot_ref[...] = _dot_nt(wt_ref[...], a)


def _project_dt(a, w, wt, tm=1024):
    t, k = a.shape
    n = w.shape[1]
    return pl.pallas_call(
        _dt_kernel,
        grid=(t // tm,),
        in_specs=[pl.BlockSpec((tm, k), lambda i: (i, 0)),
                  pl.BlockSpec((k, n), lambda i: (0, 0)),
                  pl.BlockSpec((n, k), lambda i: (0, 0))],
        out_specs=[pl.BlockSpec((tm, n), lambda i: (i, 0)), pl.BlockSpec((n, tm), lambda i: (0, i))],
        out_shape=[jax.ShapeDtypeStruct((t, n), F32), jax.ShapeDtypeStruct((n, t), F32)],
        compiler_params=_params("parallel"),
        name="proj_dt",
    )(a, w, wt)


def _moba_kernel(slopes_ref, qt_ref, k_ref, vt_ref, o_ref, kaug_ref, qaug_ref, kmh_ref, kml_ref, sel_ref, acc_ref,
                 sd_ref, s0_ref, s1_ref):
    hg = pl.program_id(1)
    i = pl.program_id(2)
    blk = MOBA_BLOCK
    dh = ATTN_HEAD_DIM
    seq = k_ref.shape[0]
    nb = seq // blk
    n_chain = qt_ref.shape[0] // dh
    n_split = SLOPE_TERMS

    @pl.when(i == 0)
    def _():
        r = lax.broadcasted_iota(jnp.int32, (nb, seq), 0)
        c = lax.broadcasted_iota(jnp.int32, (nb, seq), 1)
        lo = r * blk
        pool = jnp.where((c >= lo) & (c < lo + blk), 1.0 / blk, 0.0).astype(BF16)
        k_lane = lax.broadcasted_iota(jnp.int32, (blk, LANES), 1)
        k_row = lax.broadcasted_iota(jnp.int32, (blk, LANES), 0).astype(F32)
        q_row = lax.broadcasted_iota(jnp.int32, (LANES, blk), 0)
        q_lane = lax.broadcasted_iota(jnp.int32, (LANES, blk), 1).astype(F32)
        for g in range(n_chain):
            kg = k_ref[:, g * dh:(g + 1) * dh]
            kmean = _dot(pool, kg)
            hi = kmean.astype(BF16)
            kmh_ref[g] = hi
            kml_ref[g] = (kmean - hi.astype(F32)).astype(BF16)
            kaug_ref[g, :, 0:dh] = kg
            k_aug = jnp.where((k_lane >= n_split) & (k_lane < 2 * n_split), k_row, 0.0)
            q_aug = jnp.where(q_row < n_split, -q_lane, 0.0)
            for t in range(n_split):
                s_t = slopes_ref[(hg * n_chain + g) * n_split + t]
                k_aug = jnp.where(k_lane == t, s_t, k_aug)
                q_aug = jnp.where(q_row == n_split + t, s_t, q_aug)
            k_aug = k_aug.astype(BF16)
            for jb in range(nb):
                kaug_ref[g, jb * blk:(jb + 1) * blk, dh:dh + LANES] = k_aug
            qaug_ref[g, dh:dh + LANES, :] = q_aug.astype(BF16)

    chains = range(n_chain)

    def score_stage(j, buf_ref):
        jc = jnp.minimum(j, i)
        for g in chains:
            kb = kaug_ref[g, pl.ds(pl.multiple_of(jc * blk, blk), blk), :]
            buf_ref[g] = _dot(kb, qaug_ref[g])

    def values(g, j):
        jc = jnp.minimum(j, i)
        return vt_ref[g * dh:(g + 1) * dh, pl.ds(pl.multiple_of(jc * blk, blk), blk)]

    def softmax_stage(j, buf_ref, ms, ls):
        far = ((i - j) * blk).astype(F32)
        ms_new, ls_new = [], []
        for g in chains:
            m, l = ms[g], ls[g]
            c_shift = slopes_ref[(ATTN_HEADS + hg * n_chain + g) * n_split] * far
            keep = sel_ref[g, pl.ds(jnp.minimum(j, nb - 1), 1), :] > 0.0
            m_new = jnp.maximum(m, jnp.where(keep, jnp.max(buf_ref[g], axis=0, keepdims=True) - c_shift, NEG_INF))
            alpha = jnp.exp2(m - m_new)
            p = jnp.exp2(buf_ref[g] - jnp.where(keep, m_new + c_shift, -NEG_INF))
            ms_new.append(m_new)
            ls_new.append(alpha * l + jnp.sum(p, axis=0, keepdims=True))
            acc_ref[g] = alpha * acc_ref[g] + _dot(values(g, j), p.astype(BF16))
        return tuple(ms_new), tuple(ls_new)

    for g in chains:
        qaug_ref[g, 0:dh, :] = qt_ref[g * dh:(g + 1) * dh, :]
    score_stage(i, sd_ref)
    score_stage(0, s0_ref)

    jidx = lax.broadcasted_iota(jnp.int32, (nb, blk), 0)
    for g in chains:
        qt = qt_ref[g * dh:(g + 1) * dh, :]
        gate = _dot(kmh_ref[g], qt) + _dot(kml_ref[g], qt)
        cnt = jnp.zeros((nb, blk), F32)
        for jp in range(nb):
            row = gate[jp:jp + 1, :]
            beats = (row > gate) | ((row == gate) & (jidx > jp))
            cnt = cnt + jnp.where(beats, (jp < i).astype(F32), 0.0)
        sel_ref[g] = jnp.where((cnt < MOBA_TOPK) & (jidx < i), 1.0, 0.0)

    kl = lax.broadcasted_iota(jnp.int32, (blk, blk), 0)
    ql = lax.broadcasted_iota(jnp.int32, (blk, blk), 1)
    ms, ls = [], []
    for g in chains:
        s = jnp.where(kl <= ql, sd_ref[g], NEG_INF)
        m0 = jnp.max(s, axis=0, keepdims=True)
        p = jnp.exp2(s - m0)
        ms.append(m0)
        ls.append(jnp.sum(p, axis=0, keepdims=True))
        acc_ref[g] = _dot(values(g, i), p.astype(BF16))

    def body(t, carry):
        ms, ls = carry
        j = 2 * t
        score_stage(j + 1, s1_ref)
        ms, ls = softmax_stage(j, s0_ref, ms, ls)
        score_stage(j + 2, s0_ref)
        return softmax_stage(j + 1, s1_ref, ms, ls)

    _, ls = lax.fori_loop(0, (i + 1) // 2, body, (tuple(ms), tuple(ls)))
    for g in chains:
        o_ref[:, g * dh:(g + 1) * dh] = (acc_ref[g] / ls[g]).T.astype(o_ref.dtype)


def _moba_attention(qt, k, vt, slope_table, bsz, seq, heads_per_step=4):
    assert seq % MOBA_BLOCK == 0 and ATTN_HEADS % heads_per_step == 0
    nq = seq // MOBA_BLOCK
    nb = nq
    dh = ATTN_HEAD_DIM
    g = heads_per_step
    return pl.pallas_call(
        _moba_kernel,
        grid=(bsz, ATTN_HEADS // g, nq),
        in_specs=[pl.BlockSpec(memory_space=pltpu.SMEM),
                  pl.BlockSpec((g * dh, MOBA_BLOCK), lambda b, h, i: (h, b * nq + i)),
                  pl.BlockSpec((seq, g * dh), lambda b, h, i: (b, h)),
                  pl.BlockSpec((g * dh, seq), lambda b, h, i: (h, b))],
        out_specs=pl.BlockSpec((MOBA_BLOCK, g * dh), lambda b, h, i: (b * nq + i, h)),
        out_shape=jax.ShapeDtypeStruct((bsz * seq, ATTN_WIDTH), BF16),
        scratch_shapes=[pltpu.VMEM((g, seq, dh + LANES), BF16),
                        pltpu.VMEM((g, dh + LANES, MOBA_BLOCK), BF16),
                        pltpu.VMEM((g, nb, dh), BF16), pltpu.VMEM((g, nb, dh), BF16),
                        pltpu.VMEM((g, nb, MOBA_BLOCK), F32),
                        pltpu.VMEM((g, dh, MOBA_BLOCK), F32)]
                       + [pltpu.VMEM((g, MOBA_BLOCK, MOBA_BLOCK), F32)] * 3,
        compiler_params=_params("parallel", "parallel", "arbitrary"),
        name="moba_attention",
    )(slope_table, qt, k, vt)


def _slope_table():
    slopes = jnp.exp2(-8.0 * jnp.arange(1, ATTN_HEADS + 1, dtype=F32) / ATTN_HEADS) * LOG2E
    terms, rest = [], slopes
    for _ in range(SLOPE_TERMS):
        t = rest.astype(BF16).astype(F32)
        terms.append(t)
        rest = rest - t
    split = jnp.stack(terms, axis=1).reshape(-1)
    full = jnp.stack([slopes] * SLOPE_TERMS, axis=1).reshape(-1)
    return jnp.concatenate([split, full])


def _split3(x):
    h1 = x.astype(BF16)
    r1 = x - h1.astype(F32)
    h2 = r1.astype(BF16)
    h3 = (r1 - h2.astype(F32)).astype(BF16)
    return h1, h2, h3


def _softplus(x):
    return jnp.maximum(x, 0.0) + jnp.log1p(jnp.exp(-jnp.abs(x)))


def _ssd_kernel(act_ref, z_ref, dt_ref, dtt_ref, dtb_ref, dtbt_ref, alog_ref, alogt_ref,
                dskip_ref, normw_ref, o_ref, state_ref, y_ref):
    c = pl.program_id(1)
    ln = SSM_CHUNK

    @pl.when(c == 0)
    def _():
        state_ref[...] = jnp.zeros_like(state_ref)

    dt = _softplus(dt_ref[...] + dtb_ref[...])
    dtt = _softplus(dtt_ref[...] + dtbt_ref[...])
    la = dt * (-jnp.exp(alog_ref[...]))
    lat = dtt * (-jnp.exp(alogt_ref[...]))
    row = lax.broadcasted_iota(jnp.int32, (ln, ln), 0)
    col = lax.broadcasted_iota(jnp.int32, (ln, ln), 1)
    causal = row >= col
    tri = jnp.where(causal, 1.0, 0.0).astype(BF16)
    trit = jnp.where(row <= col, 1.0, 0.0).astype(BF16)
    a1, a2, a3 = _split3(la)
    acum = _dot(tri, a1) + _dot(tri, a2) + _dot(tri, a3)
    b1, b2, b3 = _split3(lat)
    acumt = _dot(b1, trit) + _dot(b2, trit) + _dot(b3, trit)

    left = lax.broadcasted_iota(jnp.int32, (1, LANES), 1) < SSM_HEAD_DIM
    pairs_per_group = GROUP_WIDTH // LANES

    for g in range(SSM_GROUPS):
        b0 = SSM_INNER + g * SSM_STATE
        c0 = SSM_INNER + SSM_BC + g * SSM_STATE
        bgb = act_ref[:, b0:b0 + SSM_STATE]
        cgb = act_ref[:, c0:c0 + SSM_STATE]
        cb = _dot_nt(cgb, bgb)
        bgt = bgb.astype(F32).T
        for pp in range(pairs_per_group):
            p = g * pairs_per_group + pp
            x2 = act_ref[:, p * LANES:(p + 1) * LANES]
            lhs = []
            eacs = []
            for hh in (2 * p, 2 * p + 1):
                colb = jnp.broadcast_to(acum[:, hh:hh + 1], (ln, ln))
                rowa = acumt[hh:hh + 1, :]
                dtr = dtt[hh:hh + 1, :]
                decay = jnp.exp(jnp.where(causal, colb - rowa, -jnp.inf))
                lhs.append((cb * decay * dtr).astype(BF16))
                wt = dtr * jnp.exp(acumt[hh:hh + 1, ln - 1:ln] - rowa)
                lhs.append((bgt * wt).astype(BF16))
                eacs.append(jnp.exp(colb))
            res = _dot(jnp.concatenate(lhs, axis=0), x2)
            res = jnp.where(left, res[0:2 * ln], res[2 * ln:4 * ln])
            eac = jnp.where(left, eacs[0], eacs[1])
            st = state_ref[p]
            y = res[0:ln] + _dot(cgb, st.astype(BF16)) * eac
            y_ref[:, p * LANES:(p + 1) * LANES] = y
            state_ref[p] = st * eac[ln - 1:ln, :] + res[ln:2 * ln]
        g0 = g * GROUP_WIDTH
        xg = act_ref[:, g0:g0 + GROUP_WIDTH].astype(F32)
        yg = y_ref[:, g0:g0 + GROUP_WIDTH] + xg * dskip_ref[:, g0:g0 + GROUP_WIDTH]
        yg = yg * z_ref[:, g0:g0 + GROUP_WIDTH].astype(F32)
        ms = jnp.mean(yg * yg, axis=-1, keepdims=True)
        o_ref[:, g0:g0 + GROUP_WIDTH] = (yg * lax.rsqrt(ms + EPS) * normw_ref[:, g0:g0 + GROUP_WIDTH]).astype(o_ref.dtype)


def _ssd_branch(act, zact, dt, dtt, dt_bias, a_log, d_skip, norm_w, bsz, seq):
    assert seq % SSM_CHUNK == 0
    nc = seq // SSM_CHUNK
    ln = SSM_CHUNK
    hs = SSM_HEADS
    row = lambda b, c: (b * nc + c, 0)
    fixed = lambda b, c: (0, 0)
    return pl.pallas_call(
        _ssd_kernel,
        grid=(bsz, nc),
        in_specs=[pl.BlockSpec((ln, SSM_CONV_DIM), row),
                  pl.BlockSpec((ln, SSM_INNER), row),
                  pl.BlockSpec((ln, hs), row),
                  pl.BlockSpec((hs, ln), lambda b, c: (0, b * nc + c)),
                  pl.BlockSpec((1, hs), fixed),
                  pl.BlockSpec((hs, 1), fixed),
                  pl.BlockSpec((1, hs), fixed),
                  pl.BlockSpec((hs, 1), fixed),
                  pl.BlockSpec((1, SSM_INNER), fixed),
                  pl.BlockSpec((1, SSM_INNER), fixed)],
        out_specs=pl.BlockSpec((ln, SSM_INNER), row),
        out_shape=jax.ShapeDtypeStruct((bsz * seq, SSM_INNER), BF16),
        scratch_shapes=[pltpu.VMEM((SSM_HEADS // 2, SSM_STATE, LANES), F32),
                        pltpu.VMEM((ln, SSM_INNER), F32)],
        compiler_params=_params("arbitrary", "arbitrary"),
        name="ssd",
    )(act, zact, dt, dtt, dt_bias.reshape(1, hs), dt_bias.reshape(hs, 1),
      a_log.reshape(1, hs), a_log.reshape(hs, 1),
      jnp.repeat(d_skip, SSM_HEAD_DIM).reshape(1, SSM_INNER), norm_w.reshape(1, SSM_INNER))


def _merge_kernel(attn_ref, ssm_ref, ga_ref, gs_ref, wa_ref, ws_ref, o_ref):
    ya = _dot(attn_ref[...], wa_ref[...])
    ys = _dot(ssm_ref[...], ws_ref[...])
    o_ref[...] = (ga_ref[...].astype(F32) * ya + gs_ref[...].astype(F32) * ys).astype(o_ref.dtype)


def _merge(attn, ssm, gate, wa, ws, tm=1024, tn=512):
    t = attn.shape[0]
    n = wa.shape[1]
    nj = n // tn
    return pl.pallas_call(
        _merge_kernel,
        grid=(t // tm, nj),
        in_specs=[pl.BlockSpec((tm, attn.shape[1]), lambda i, j: (i, 0)),
                  pl.BlockSpec((tm, ssm.shape[1]), lambda i, j: (i, 0)),
                  pl.BlockSpec((tm, tn), lambda i, j: (i, j)),
                  pl.BlockSpec((tm, tn), lambda i, j: (i, j + nj)),
                  pl.BlockSpec((wa.shape[0], tn), lambda i, j: (0, j)),
                  pl.BlockSpec((ws.shape[0], tn), lambda i, j: (0, j))],
        out_specs=pl.BlockSpec((tm, tn), lambda i, j: (i, j)),
        out_shape=jax.ShapeDtypeStruct((t, n), BF16),
        compiler_params=_params("parallel", "parallel"),
        name="merge",
    )(attn, ssm, gate, gate, wa, ws)


def _out_mlp_kernel(x_ref, m_ref, wo_ref, nw_ref, wu_ref, wd_ref, o_ref, hn_ref):
    j = pl.program_id(1)

    @pl.when(j == 0)
    def _():
        h = x_ref[...] + _dot(m_ref[...], wo_ref[...])
        ms = jnp.mean(h * h, axis=-1, keepdims=True)
        hn_ref[...] = (h * lax.rsqrt(ms + EPS) * nw_ref[...]).astype(hn_ref.dtype)
        o_ref[...] = h

    u = _dot(hn_ref[...], wu_ref[...])
    act = jnp.square(jnp.maximum(u, 0.0)).astype(BF16)
    o_ref[...] += _dot(act, wd_ref[...])


def _out_mlp(x, merged, w_out, norm_w, w_up, w_down, tm=512, tf=512):
    t, d = x.shape
    f = w_up.shape[1]
    return pl.pallas_call(
        _out_mlp_kernel,
        grid=(t // tm, f // tf),
        in_specs=[pl.BlockSpec((tm, d), lambda i, j: (i, 0)),
                  pl.BlockSpec((tm, merged.shape[1]), lambda i, j: (i, 0)),
                  pl.BlockSpec(w_out.shape, lambda i, j: (0, 0)),
                  pl.BlockSpec((1, d), lambda i, j: (0, 0)),
                  pl.BlockSpec((d, tf), lambda i, j: (0, j)),
                  pl.BlockSpec((tf, d), lambda i, j: (j, 0))],
        out_specs=pl.BlockSpec((tm, d), lambda i, j: (i, 0)),
        out_shape=jax.ShapeDtypeStruct((t, d), F32),
        scratch_shapes=[pltpu.VMEM((tm, d), BF16)],
        compiler_params=_params("parallel", "arbitrary"),
        name="out_mlp",
    )(x, merged, w_out, norm_w.reshape(1, d), w_up, w_down)


def _layer(h, bsz, seq, mix_norm_w, w_in, q_norm_w, k_norm_w, conv_w, conv_b, dt_bias, a_log, d_skip,
           ssm_norm_w, w_attn_out, w_ssm_out, w_out, mlp_norm_w, w_up, w_down, slopes):
    o_k = ATTN_WIDTH
    o_v = 2 * ATTN_WIDTH
    o_z = 3 * ATTN_WIDTH
    o_x = o_z + SSM_INNER
    o_dt = o_x + SSM_CONV_DIM
    o_g = o_dt + SSM_HEADS
    assert o_g - o_dt == GATE_SKEW and o_dt % PROJ_TN == 0
    w_t = jnp.swapaxes(w_in, 0, 1)
    wdt_t = w_t[o_dt:o_g].astype(BF16)
    head_col = pl.BlockSpec((ATTN_HEAD_DIM, 1), lambda j, i: (0, 0))
    head_row = pl.BlockSpec((1, ATTN_HEAD_DIM), lambda j, i: (0, 0))
    tn = PROJ_TN
    skew_blocks = tn // GATE_SKEW

    hn = _rmsnorm(h, mix_norm_w)
    qt = _project(hn, w_t, _proj_qnorm_t_kernel, n=ATTN_WIDTH, transposed=True,
                  extra=(q_norm_w.reshape(ATTN_HEAD_DIM, 1),), extra_specs=(head_col,), name="proj_q")
    k = _project(hn, w_t, _proj_knorm_kernel, n=ATTN_WIDTH, col0=o_k,
                 extra=(k_norm_w.reshape(1, ATTN_HEAD_DIM),), extra_specs=(head_row,), name="proj_k")
    vt = _project(hn, w_t, _proj_t_kernel, n=ATTN_WIDTH, col0=o_v, transposed=True, name="proj_v")
    zact = _project(hn, w_t, functools.partial(_proj_act_kernel, act=_silu), n=SSM_INNER, col0=o_z, name="proj_z")
    act = _project(hn, w_t, functools.partial(_proj_conv_kernel, seq=seq, sub=CONV_SUB), n=SSM_CONV_DIM, col0=o_x,
                   extra=(conv_w, conv_b.reshape(1, -1)),
                   extra_specs=(pl.BlockSpec((SSM_CONV, tn), lambda j, i: (0, j)),
                                pl.BlockSpec((1, tn), lambda j, i: (0, j))),
                   scratch=(pltpu.VMEM((CONV_HALO, tn), F32),), name="proj_xbc")
    gates = _project(hn, w_t, _proj_gate_kernel, n=N_GATES, col0=o_dt, extra=(w_t,),
                     extra_specs=(pl.BlockSpec((GATE_SKEW, w_t.shape[1]),
                                               lambda j, i: (o_dt // GATE_SKEW + (j + 1) * skew_blocks, 0)),),
                     name="proj_gate")
    dt, dtt = _project_dt(hn, wdt_t.T, wdt_t)

    attn = _moba_attention(qt, k, vt, slopes, bsz, seq)
    ssm = _ssd_branch(act, zact, dt, dtt, dt_bias, a_log, d_skip, ssm_norm_w, bsz, seq)
    merged = _merge(attn, ssm, gates, w_attn_out.astype(BF16), w_ssm_out.astype(BF16))
    return _out_mlp(h, merged, w_out.astype(BF16), mlp_norm_w, w_up.astype(BF16), w_down.astype(BF16))


def kernel(x, mix_norm_w, w_in, q_norm_w, k_norm_w, conv_w, conv_b, dt_bias, a_log, d_skip, ssm_norm_w,
           w_attn_out, w_ssm_out, w_out, mlp_norm_w, w_up, w_down):
    bsz, seq, d = x.shape
    slopes = _slope_table()
    h = x.reshape(bsz * seq, d)
    for layer in range(w_in.shape[0]):
        h = _layer(h, bsz, seq, mix_norm_w[layer], w_in[layer], q_norm_w[layer], k_norm_w[layer], conv_w[layer],
                   conv_b[layer], dt_bias[layer], a_log[layer], d_skip[layer], ssm_norm_w[layer], w_attn_out[layer],
                   w_ssm_out[layer], w_out[layer], mlp_norm_w[layer], w_up[layer], w_down[layer], slopes)
    return h.reshape(bsz, seq, d)
```

```python
import functools

import jax
import jax.numpy as jnp
from jax import lax
from jax.experimental import pallas as pl
from jax.experimental.pallas import tpu as pltpu

D_MODEL = 2048
ATTN_HEAD_DIM = 128
ATTN_HEADS = D_MODEL // ATTN_HEAD_DIM
ATTN_WIDTH = ATTN_HEADS * ATTN_HEAD_DIM
MOBA_BLOCK = 256
MOBA_TOPK = 3
SSM_INNER = 2 * D_MODEL
SSM_HEAD_DIM = 64
SSM_HEADS = SSM_INNER // SSM_HEAD_DIM
SSM_GROUPS = 8
SSM_STATE = 128
SSM_CONV = 4
SSM_CHUNK = 128
SSM_BC = SSM_GROUPS * SSM_STATE
SSM_CONV_DIM = SSM_INNER + 2 * SSM_BC
GROUP_WIDTH = SSM_INNER // SSM_GROUPS
D_FF = 4 * D_MODEL
EPS = 1e-6
NEG_INF = -1e30
LOG2E = 1.4426950408889634
Q_SCALE = ATTN_HEAD_DIM ** -0.5 * LOG2E
SLOPE_TERMS = 3
AUG_COLS = 16
SUM_ROWS = 16

LANES = 128
CONV_HALO = 8
PROJ_TM = 1024
PROJ_TN = 1024
CONV_SUB = 256
N_GATES = 2 * D_MODEL
GATE_SKEW = SSM_HEADS
VMEM_LIMIT = 56 * 1024 * 1024

F32 = jnp.float32
BF16 = jnp.bfloat16

_NT = (((1,), (1,)), ((), ()))


def _params(*sem):
    return pltpu.CompilerParams(dimension_semantics=sem, vmem_limit_bytes=VMEM_LIMIT)


def _dot(a, b):
    return jnp.dot(a, b, preferred_element_type=F32)


def _dot_nt(a, b):
    return lax.dot_general(a, b, _NT, preferred_element_type=F32)


def _rmsnorm_dt_kernel(x_ref, w_ref, wdt_ref, wdtt_ref, o_ref, dt_ref, dtt_ref):
    x = x_ref[...]
    ms = jnp.mean(x * x, axis=-1, keepdims=True)
    hn = (x * lax.rsqrt(ms + EPS) * w_ref[...]).astype(o_ref.dtype)
    o_ref[...] = hn
    dt_ref[...] = _dot(hn, wdt_ref[...])
    dtt_ref[...] = _dot_nt(wdtt_ref[...], hn)


def _rmsnorm_dt(x, w, wdt, wdt_t, tm=512):
    t, d = x.shape
    n = wdt.shape[1]
    return pl.pallas_call(
        _rmsnorm_dt_kernel,
        grid=(t // tm,),
        in_specs=[pl.BlockSpec((tm, d), lambda i: (i, 0)), pl.BlockSpec((1, d), lambda i: (0, 0)),
                  pl.BlockSpec((d, n), lambda i: (0, 0)), pl.BlockSpec((n, d), lambda i: (0, 0))],
        out_specs=[pl.BlockSpec((tm, d), lambda i: (i, 0)), pl.BlockSpec((tm, n), lambda i: (i, 0)),
                   pl.BlockSpec((n, tm), lambda i: (0, i))],
        out_shape=[jax.ShapeDtypeStruct((t, d), BF16), jax.ShapeDtypeStruct((t, n), F32),
                   jax.ShapeDtypeStruct((n, t), F32)],
        compiler_params=_params("parallel"),
        name="rmsnorm_dt",
    )(x, w.reshape(1, d), wdt, wdt_t)


def _stage_weight(w_ref, wbf_ref, transpose=True):
    @pl.when(pl.program_id(1) == 0)
    def _():
        w = w_ref[...]
        wbf_ref[...] = (w.T if transpose else w).astype(BF16)


def _silu(x):
    half = 0.5 * x
    return half + half * jnp.tanh(half)


def _proj_act_kernel(a_ref, w_ref, o_ref, wbf_ref, *, act):
    _stage_weight(w_ref, wbf_ref)
    o_ref[...] = act(_dot(a_ref[...], wbf_ref[...])).astype(o_ref.dtype)


def _proj_gate_kernel(a_ref, w_ref, wnext_ref, o_ref, wbf_ref):
    @pl.when(pl.program_id(1) == 0)
    def _():
        w = jnp.concatenate([w_ref[GATE_SKEW:, :], wnext_ref[...]], axis=0)
        wbf_ref[...] = w.T.astype(BF16)

    o_ref[...] = jax.nn.sigmoid(_dot(a_ref[...], wbf_ref[...])).astype(o_ref.dtype)


def _proj_t_kernel(a_ref, wt_ref, o_ref, wbf_ref):
    _stage_weight(wt_ref, wbf_ref, transpose=False)
    o_ref[...] = _dot_nt(wbf_ref[...], a_ref[...]).astype(o_ref.dtype)


def _proj_knorm_kernel(a_ref, w_ref, nw_ref, o_ref, wbf_ref):
    _stage_weight(w_ref, wbf_ref)
    acc = _dot(a_ref[...], wbf_ref[...])
    nw = nw_ref[...]
    for h in range(acc.shape[1] // ATTN_HEAD_DIM):
        slab = acc[:, h * ATTN_HEAD_DIM:(h + 1) * ATTN_HEAD_DIM]
        ms = jnp.mean(slab * slab, axis=-1, keepdims=True)
        o_ref[:, h * ATTN_HEAD_DIM:(h + 1) * ATTN_HEAD_DIM] = (slab * lax.rsqrt(ms + EPS) * nw).astype(o_ref.dtype)


def _proj_qnorm_t_kernel(a_ref, wt_ref, nw_ref, o_ref, wbf_ref):
    _stage_weight(wt_ref, wbf_ref, transpose=False)
    acc = _dot_nt(wbf_ref[...], a_ref[...])
    nw = nw_ref[...]
    for h in range(acc.shape[0] // ATTN_HEAD_DIM):
        slab = acc[h * ATTN_HEAD_DIM:(h + 1) * ATTN_HEAD_DIM, :]
        ms = jnp.mean(slab * slab, axis=0, keepdims=True)
        qn = slab * lax.rsqrt(ms + EPS) * nw
        o_ref[h * ATTN_HEAD_DIM:(h + 1) * ATTN_HEAD_DIM, :] = (qn * Q_SCALE).astype(o_ref.dtype)


def _proj_conv_kernel(a_ref, w_ref, cw_ref, cb_ref, o_ref, wbf_ref, tail_ref, *, seq, sub):
    _stage_weight(w_ref, wbf_ref)
    i = pl.program_id(1)
    tm = a_ref.shape[0]
    w = wbf_ref[...]

    @pl.when((i * tm) % seq == 0)
    def _():
        tail_ref[...] = jnp.zeros_like(tail_ref)

    tail = tail_ref[...]
    for r in range(0, tm, sub):
        cur = _dot(a_ref[r:r + sub, :], w)
        ext = jnp.concatenate([tail, cur], axis=0)
        acc = cb_ref[...] + cw_ref[SSM_CONV - 1:SSM_CONV, :] * cur
        for back in range(1, SSM_CONV):
            shifted = pltpu.roll(ext, back, axis=0)[CONV_HALO:CONV_HALO + sub, :]
            acc = acc + cw_ref[SSM_CONV - 1 - back:SSM_CONV - back, :] * shifted
        o_ref[r:r + sub, :] = _silu(acc).astype(o_ref.dtype)
        tail = cur[sub - CONV_HALO:sub, :]
    tail_ref[...] = tail


def _project(a, w, body, *, n, col0=0, transposed=False, extra=(), extra_specs=(), scratch=(),
             out_dtype=BF16, tm=PROJ_TM, tn=PROJ_TN, name="proj"):
    t, k = a.shape
    assert n % tn == 0 and col0 % tn == 0 and t % tm == 0
    c0 = col0 // tn
    in_specs = [pl.BlockSpec((tm, k), lambda j, i: (i, 0)), pl.BlockSpec((tn, k), lambda j, i: (c0 + j, 0))]
    args = [a, w]
    if transposed:
        o_spec = pl.BlockSpec((tn, tm), lambda j, i: (j, i))
        o_shape = (n, t)
        w_block = (tn, k)
    else:
        o_spec = pl.BlockSpec((tm, tn), lambda j, i: (i, j))
        o_shape = (t, n)
        w_block = (k, tn)
    return pl.pallas_call(
        body,
        grid=(n // tn, t // tm),
        in_specs=in_specs + list(extra_specs),
        out_specs=o_spec,
        out_shape=jax.ShapeDtypeStruct(o_shape, out_dtype),
        scratch_shapes=[pltpu.VMEM(w_block, BF16)] + list(scratch),
        compiler_params=_params("parallel", "arbitrary"),
        name=name,
    )(*args, *extra)


def _moba_kernel(slopes_ref, qt_ref, k_ref, vt_ref, o_ref, kaug_ref, qaug_ref, kmh_ref, kml_ref, sel_ref, acc_ref,
                 sd_ref, s0_ref, s1_ref, vaug_ref):
    hg = pl.program_id(1)
    i = pl.program_id(2)
    blk = MOBA_BLOCK
    dh = ATTN_HEAD_DIM
    seq = k_ref.shape[0]
    nb = seq // blk
    n_chain = qt_ref.shape[0] // dh
    n_split = SLOPE_TERMS

    @pl.when(i == 0)
    def _():
        r = lax.broadcasted_iota(jnp.int32, (nb, seq), 0)
        c = lax.broadcasted_iota(jnp.int32, (nb, seq), 1)
        lo = r * blk
        pool = jnp.where((c >= lo) & (c < lo + blk), 1.0 / blk, 0.0).astype(BF16)
        k_lane = lax.broadcasted_iota(jnp.int32, (blk, AUG_COLS), 1)
        k_row = lax.broadcasted_iota(jnp.int32, (blk, AUG_COLS), 0).astype(F32)
        q_row = lax.broadcasted_iota(jnp.int32, (AUG_COLS, blk), 0)
        q_lane = lax.broadcasted_iota(jnp.int32, (AUG_COLS, blk), 1).astype(F32)
        ones_row = lax.broadcasted_iota(jnp.int32, (SUM_ROWS, seq), 0) == 0
        for g in range(n_chain):
            vaug_ref[g, 0:dh, :] = vt_ref[g * dh:(g + 1) * dh, :]
            vaug_ref[g, dh:dh + SUM_ROWS, :] = jnp.where(ones_row, 1.0, 0.0).astype(BF16)
            kg = k_ref[:, g * dh:(g + 1) * dh]
            kmean = _dot(pool, kg)
            hi = kmean.astype(BF16)
            kmh_ref[g] = hi
            kml_ref[g] = (kmean - hi.astype(F32)).astype(BF16)
            kaug_ref[g, :, 0:dh] = kg
            k_aug = jnp.where((k_lane >= n_split) & (k_lane < 2 * n_split), k_row, 0.0)
            q_aug = jnp.where(q_row < n_split, -q_lane, 0.0)
            for t in range(n_split):
                s_t = slopes_ref[(hg * n_chain + g) * n_split + t]
                k_aug = jnp.where(k_lane == t, s_t, k_aug)
                q_aug = jnp.where(q_row == n_split + t, s_t, q_aug)
            k_aug = k_aug.astype(BF16)
            for jb in range(nb):
                kaug_ref[g, jb * blk:(jb + 1) * blk, dh:dh + AUG_COLS] = k_aug
            qaug_ref[g, dh:dh + AUG_COLS, :] = q_aug.astype(BF16)

    chains = range(n_chain)

    def score_stage(j, buf_ref):
        jc = jnp.minimum(j, i)
        for g in chains:
            kb = kaug_ref[g, pl.ds(pl.multiple_of(jc * blk, blk), blk), :]
            buf_ref[g] = _dot(kb, qaug_ref[g])

    def values(g, j):
        jc = jnp.minimum(j, i)
        return vaug_ref[g, :, pl.ds(pl.multiple_of(jc * blk, blk), blk)]

    def softmax_stage(j, buf_ref, ms):
        far = ((i - j) * blk).astype(F32)
        ms_new = []
        for g in chains:
            m = ms[g]
            c_shift = slopes_ref[(ATTN_HEADS + hg * n_chain + g) * n_split] * far
            keep = sel_ref[g, pl.ds(jnp.minimum(j, nb - 1), 1), :] > 0.0
            m_new = jnp.maximum(m, jnp.where(keep, jnp.max(buf_ref[g], axis=0, keepdims=True) - c_shift, NEG_INF))
            alpha = jnp.exp2(m - m_new)
            p = jnp.exp2(buf_ref[g] - jnp.where(keep, m_new + c_shift, -NEG_INF))
            ms_new.append(m_new)
            acc_ref[g] = alpha * acc_ref[g] + _dot(values(g, j), p.astype(BF16))
        return tuple(ms_new)

    for g in chains:
        qaug_ref[g, 0:dh, :] = qt_ref[g * dh:(g + 1) * dh, :]
    score_stage(i, sd_ref)
    score_stage(0, s0_ref)

    jidx = lax.broadcasted_iota(jnp.int32, (nb, blk), 0)
    for g in chains:
        qt = qt_ref[g * dh:(g + 1) * dh, :]
        gate = _dot(kmh_ref[g], qt) + _dot(kml_ref[g], qt)
        cnt = jnp.zeros((nb, blk), F32)
        for jp in range(nb):
            row = gate[jp:jp + 1, :]
            beats = (row > gate) | ((row == gate) & (jidx > jp))
            cnt = cnt + jnp.where(beats, (jp < i).astype(F32), 0.0)
        sel_ref[g] = jnp.where((cnt < MOBA_TOPK) & (jidx < i), 1.0, 0.0)

    kl = lax.broadcasted_iota(jnp.int32, (blk, blk), 0)
    ql = lax.broadcasted_iota(jnp.int32, (blk, blk), 1)
    ms = []
    for g in chains:
        s = jnp.where(kl <= ql, sd_ref[g], NEG_INF)
        m0 = jnp.max(s, axis=0, keepdims=True)
        p = jnp.exp2(s - m0)
        ms.append(m0)
        acc_ref[g] = _dot(values(g, i), p.astype(BF16))

    def body(t, ms):
        j = 2 * t
        score_stage(j + 1, s1_ref)
        ms = softmax_stage(j, s0_ref, ms)
        score_stage(j + 2, s0_ref)
        return softmax_stage(j + 1, s1_ref, ms)

    lax.fori_loop(0, (i + 1) // 2, body, tuple(ms))
    for g in chains:
        acc = acc_ref[g]
        o_ref[:, g * dh:(g + 1) * dh] = (acc[0:dh] / acc[dh:dh + 1]).T.astype(o_ref.dtype)


def _moba_attention(qt, k, vt, slope_table, bsz, seq, heads_per_step=4):
    assert seq % MOBA_BLOCK == 0 and ATTN_HEADS % heads_per_step == 0
    nq = seq // MOBA_BLOCK
    nb = nq
    dh = ATTN_HEAD_DIM
    g = heads_per_step
    return pl.pallas_call(
        _moba_kernel,
        grid=(bsz, ATTN_HEADS // g, nq),
        in_specs=[pl.BlockSpec(memory_space=pltpu.SMEM),
                  pl.BlockSpec((g * dh, MOBA_BLOCK), lambda b, h, i: (h, b * nq + i)),
                  pl.BlockSpec((seq, g * dh), lambda b, h, i: (b, h)),
                  pl.BlockSpec((g * dh, seq), lambda b, h, i: (h, b))],
        out_specs=pl.BlockSpec((MOBA_BLOCK, g * dh), lambda b, h, i: (b * nq + i, h)),
        out_shape=jax.ShapeDtypeStruct((bsz * seq, ATTN_WIDTH), BF16),
        scratch_shapes=[pltpu.VMEM((g, seq, dh + AUG_COLS), BF16),
                        pltpu.VMEM((g, dh + AUG_COLS, MOBA_BLOCK), BF16),
                        pltpu.VMEM((g, nb, dh), BF16), pltpu.VMEM((g, nb, dh), BF16),
                        pltpu.VMEM((g, nb, MOBA_BLOCK), F32),
                        pltpu.VMEM((g, dh + SUM_ROWS, MOBA_BLOCK), F32)]
                       + [pltpu.VMEM((g, MOBA_BLOCK, MOBA_BLOCK), F32)] * 3
                       + [pltpu.VMEM((g, dh + SUM_ROWS, seq), BF16)],
        compiler_params=_params("parallel", "parallel", "arbitrary"),
        name="moba_attention",
    )(slope_table, qt, k, vt)


def _slope_table():
    slopes = jnp.exp2(-8.0 * jnp.arange(1, ATTN_HEADS + 1, dtype=F32) / ATTN_HEADS) * LOG2E
    terms, rest = [], slopes
    for _ in range(SLOPE_TERMS):
        t = rest.astype(BF16).astype(F32)
        terms.append(t)
        rest = rest - t
    split = jnp.stack(terms, axis=1).reshape(-1)
    full = jnp.stack([slopes] * SLOPE_TERMS, axis=1).reshape(-1)
    return jnp.concatenate([split, full])


def _split3(x):
    h1 = x.astype(BF16)
    r1 = x - h1.astype(F32)
    h2 = r1.astype(BF16)
    h3 = (r1 - h2.astype(F32)).astype(BF16)
    return h1, h2, h3


def _softplus(x):
    return jnp.maximum(x, 0.0) + jnp.log1p(jnp.exp(-jnp.abs(x)))


def _ssd_kernel(act_ref, z_ref, dt_ref, dtt_ref, dtb_ref, dtbt_ref, alog_ref, alogt_ref,
                dskip_ref, normw_ref, o_ref, state_ref, y_ref):
    c = pl.program_id(1)
    ln = SSM_CHUNK

    @pl.when(c == 0)
    def _():
        state_ref[...] = jnp.zeros_like(state_ref)

    dt = _softplus(dt_ref[...] + dtb_ref[...])
    dtt = _softplus(dtt_ref[...] + dtbt_ref[...])
    la = dt * (-jnp.exp(alog_ref[...]))
    lat = dtt * (-jnp.exp(alogt_ref[...]))
    row = lax.broadcasted_iota(jnp.int32, (ln, ln), 0)
    col = lax.broadcasted_iota(jnp.int32, (ln, ln), 1)
    causal = row >= col
    tri = jnp.where(causal, 1.0, 0.0).astype(BF16)
    trit = jnp.where(row <= col, 1.0, 0.0).astype(BF16)
    a1, a2, a3 = _split3(la)
    acum = (_dot(tri, a1) + _dot(tri, a2) + _dot(tri, a3)) * LOG2E
    b1, b2, b3 = _split3(lat)
    acumt = (_dot(b1, trit) + _dot(b2, trit) + _dot(b3, trit)) * LOG2E
    rowdt = acumt - jnp.log2(dtt)

    left = lax.broadcasted_iota(jnp.int32, (1, LANES), 1) < SSM_HEAD_DIM
    left_full = lax.broadcasted_iota(jnp.int32, (ln, LANES), 1) < SSM_HEAD_DIM
    pairs_per_group = GROUP_WIDTH // LANES

    for g in range(SSM_GROUPS):
        b0 = SSM_INNER + g * SSM_STATE
        c0 = SSM_INNER + SSM_BC + g * SSM_STATE
        bgb = act_ref[:, b0:b0 + SSM_STATE]
        cgb = act_ref[:, c0:c0 + SSM_STATE]
        cb = _dot_nt(cgb, bgb)
        bgt = bgb.astype(F32).T
        for pp in range(pairs_per_group):
            p = g * pairs_per_group + pp
            x2 = act_ref[:, p * LANES:(p + 1) * LANES]
            x_heads = (jnp.where(left_full, x2, 0), jnp.where(left_full, 0, x2))
            ops = []
            eacs = []
            for hh in (2 * p, 2 * p + 1):
                lhs = []
                colb = jnp.broadcast_to(acum[:, hh:hh + 1], (ln, ln))
                rowb = rowdt[hh:hh + 1, :]
                decay_dt = jnp.exp2(jnp.where(causal, colb - rowb, -jnp.inf))
                lhs.append((cb * decay_dt).astype(BF16))
                wt = jnp.exp2(acumt[hh:hh + 1, ln - 1:ln] - rowb)
                lhs.append((bgt * wt).astype(BF16))
                eacs.append(jnp.exp2(colb))
                ops.append(jnp.concatenate(lhs, axis=0))
            res = _dot(ops[0], x_heads[0]) + _dot(ops[1], x_heads[1])
            eac = jnp.where(left, eacs[0], eacs[1])
            st = state_ref[p]
            y = res[0:ln] + _dot(cgb, st.astype(BF16)) * eac
            y_ref[:, p * LANES:(p + 1) * LANES] = y
            state_ref[p] = st * eac[ln - 1:ln, :] + res[ln:2 * ln]
        g0 = g * GROUP_WIDTH
        xg = act_ref[:, g0:g0 + GROUP_WIDTH].astype(F32)
        yg = y_ref[:, g0:g0 + GROUP_WIDTH] + xg * dskip_ref[:, g0:g0 + GROUP_WIDTH]
        yg = yg * z_ref[:, g0:g0 + GROUP_WIDTH].astype(F32)
        ms = jnp.mean(yg * yg, axis=-1, keepdims=True)
        o_ref[:, g0:g0 + GROUP_WIDTH] = (yg * lax.rsqrt(ms + EPS) * normw_ref[:, g0:g0 + GROUP_WIDTH]).astype(o_ref.dtype)


def _ssd_branch(act, zact, dt, dtt, dt_bias, a_log, d_skip, norm_w, bsz, seq):
    assert seq % SSM_CHUNK == 0
    nc = seq // SSM_CHUNK
    ln = SSM_CHUNK
    hs = SSM_HEADS
    row = lambda b, c: (b * nc + c, 0)
    fixed = lambda b, c: (0, 0)
    return pl.pallas_call(
        _ssd_kernel,
        grid=(bsz, nc),
        in_specs=[pl.BlockSpec((ln, SSM_CONV_DIM), row),
                  pl.BlockSpec((ln, SSM_INNER), row),
                  pl.BlockSpec((ln, hs), row),
                  pl.BlockSpec((hs, ln), lambda b, c: (0, b * nc + c)),
                  pl.BlockSpec((1, hs), fixed),
                  pl.BlockSpec((hs, 1), fixed),
                  pl.BlockSpec((1, hs), fixed),
                  pl.BlockSpec((hs, 1), fixed),
                  pl.BlockSpec((1, SSM_INNER), fixed),
                  pl.BlockSpec((1, SSM_INNER), fixed)],
        out_specs=pl.BlockSpec((ln, SSM_INNER), row),
        out_shape=jax.ShapeDtypeStruct((bsz * seq, SSM_INNER), BF16),
        scratch_shapes=[pltpu.VMEM((SSM_HEADS // 2, SSM_STATE, LANES), F32),
                        pltpu.VMEM((ln, SSM_INNER), F32)],
        compiler_params=_params("arbitrary", "arbitrary"),
        name="ssd",
    )(act, zact, dt, dtt, dt_bias.reshape(1, hs), dt_bias.reshape(hs, 1),
      a_log.reshape(1, hs), a_log.reshape(hs, 1),
      jnp.repeat(d_skip, SSM_HEAD_DIM).reshape(1, SSM_INNER), norm_w.reshape(1, SSM_INNER))


def _merge_kernel(attn_ref, ssm_ref, ga_ref, gs_ref, wa_ref, ws_ref, o_ref, wabf_ref, wsbf_ref):
    @pl.when(pl.program_id(1) == 0)
    def _():
        wabf_ref[...] = wa_ref[...].astype(BF16)
        wsbf_ref[...] = ws_ref[...].astype(BF16)

    ya = _dot(attn_ref[...], wabf_ref[...])
    ys = _dot(ssm_ref[...], wsbf_ref[...])
    o_ref[...] = (ga_ref[...].astype(F32) * ya + gs_ref[...].astype(F32) * ys).astype(o_ref.dtype)


def _merge(attn, ssm, gate, wa, ws, tm=512, tn=512):
    t = attn.shape[0]
    n = wa.shape[1]
    nj = n // tn
    return pl.pallas_call(
        _merge_kernel,
        grid=(nj, t // tm),
        in_specs=[pl.BlockSpec((tm, attn.shape[1]), lambda j, i: (i, 0)),
                  pl.BlockSpec((tm, ssm.shape[1]), lambda j, i: (i, 0)),
                  pl.BlockSpec((tm, tn), lambda j, i: (i, j)),
                  pl.BlockSpec((tm, tn), lambda j, i: (i, j + nj)),
                  pl.BlockSpec((wa.shape[0], tn), lambda j, i: (0, j)),
                  pl.BlockSpec((ws.shape[0], tn), lambda j, i: (0, j))],
        out_specs=pl.BlockSpec((tm, tn), lambda j, i: (i, j)),
        out_shape=jax.ShapeDtypeStruct((t, n), BF16),
        scratch_shapes=[pltpu.VMEM((wa.shape[0], tn), BF16), pltpu.VMEM((ws.shape[0], tn), BF16)],
        compiler_params=_params("parallel", "arbitrary"),
        name="merge",
    )(attn, ssm, gate, gate, wa, ws)


def _out_mlp_kernel(x_ref, m_ref, wo_ref, nw_ref, wu_ref, wd_ref, o_ref, hn_ref):
    j = pl.program_id(1)

    @pl.when(j == 0)
    def _():
        h = x_ref[...] + _dot(m_ref[...], wo_ref[...])
        ms = jnp.mean(h * h, axis=-1, keepdims=True)
        hn_ref[...] = (h * lax.rsqrt(ms + EPS) * nw_ref[...]).astype(hn_ref.dtype)
        o_ref[...] = h

    u = _dot(hn_ref[...], wu_ref[...])
    act = jnp.square(jnp.maximum(u, 0.0)).astype(BF16)
    o_ref[...] += _dot(act, wd_ref[...])


def _out_mlp(x, merged, w_out, norm_w, w_up, w_down, tm=512, tf=512):
    t, d = x.shape
    f = w_up.shape[1]
    return pl.pallas_call(
        _out_mlp_kernel,
        grid=(t // tm, f // tf),
        in_specs=[pl.BlockSpec((tm, d), lambda i, j: (i, 0)),
                  pl.BlockSpec((tm, merged.shape[1]), lambda i, j: (i, 0)),
                  pl.BlockSpec(w_out.shape, lambda i, j: (0, 0)),
                  pl.BlockSpec((1, d), lambda i, j: (0, 0)),
                  pl.BlockSpec((d, tf), lambda i, j: (0, j)),
                  pl.BlockSpec((tf, d), lambda i, j: (j, 0))],
        out_specs=pl.BlockSpec((tm, d), lambda i, j: (i, 0)),
        out_shape=jax.ShapeDtypeStruct((t, d), F32),
        scratch_shapes=[pltpu.VMEM((tm, d), BF16)],
        compiler_params=_params("parallel", "arbitrary"),
        name="out_mlp",
    )(x, merged, w_out, norm_w.reshape(1, d), w_up, w_down)


def _layer(h, bsz, seq, mix_norm_w, w_in, q_norm_w, k_norm_w, conv_w, conv_b, dt_bias, a_log, d_skip,
           ssm_norm_w, w_attn_out, w_ssm_out, w_out, mlp_norm_w, w_up, w_down, slopes):
    o_k = ATTN_WIDTH
    o_v = 2 * ATTN_WIDTH
    o_z = 3 * ATTN_WIDTH
    o_x = o_z + SSM_INNER
    o_dt = o_x + SSM_CONV_DIM
    o_g = o_dt + SSM_HEADS
    assert o_g - o_dt == GATE_SKEW and o_dt % PROJ_TN == 0
    w_t = jnp.swapaxes(w_in, 0, 1)
    wdt_t = w_t[o_dt:o_g].astype(BF16)
    head_col = pl.BlockSpec((ATTN_HEAD_DIM, 1), lambda j, i: (0, 0))
    head_row = pl.BlockSpec((1, ATTN_HEAD_DIM), lambda j, i: (0, 0))
    tn = PROJ_TN
    skew_blocks = tn // GATE_SKEW

    hn, dt, dtt = _rmsnorm_dt(h, mix_norm_w, wdt_t.T, wdt_t)
    qt = _project(hn, w_t, _proj_qnorm_t_kernel, n=ATTN_WIDTH, transposed=True,
                  extra=(q_norm_w.reshape(ATTN_HEAD_DIM, 1),), extra_specs=(head_col,), name="proj_q")
    k = _project(hn, w_t, _proj_knorm_kernel, n=ATTN_WIDTH, col0=o_k,
                 extra=(k_norm_w.reshape(1, ATTN_HEAD_DIM),), extra_specs=(head_row,), name="proj_k")
    vt = _project(hn, w_t, _proj_t_kernel, n=ATTN_WIDTH, col0=o_v, transposed=True, name="proj_v")
    zact = _project(hn, w_t, functools.partial(_proj_act_kernel, act=_silu), n=SSM_INNER, col0=o_z, name="proj_z")
    act = _project(hn, w_t, functools.partial(_proj_conv_kernel, seq=seq, sub=CONV_SUB), n=SSM_CONV_DIM, col0=o_x,
                   extra=(conv_w, conv_b.reshape(1, -1)),
                   extra_specs=(pl.BlockSpec((SSM_CONV, tn), lambda j, i: (0, j)),
                                pl.BlockSpec((1, tn), lambda j, i: (0, j))),
                   scratch=(pltpu.VMEM((CONV_HALO, tn), F32),), name="proj_xbc")
    gates = _project(hn, w_t, _proj_gate_kernel, n=N_GATES, col0=o_dt, extra=(w_t,),
                     extra_specs=(pl.BlockSpec((GATE_SKEW, w_t.shape[1]),
                                               lambda j, i: (o_dt // GATE_SKEW + (j + 1) * skew_blocks, 0)),),
                     name="proj_gate")

    attn = _moba_attention(qt, k, vt, slopes, bsz, seq)
    ssm = _ssd_branch(act, zact, dt, dtt, dt_bias, a_log, d_skip, ssm_norm_w, bsz, seq)
    merged = _merge(attn, ssm, gates, w_attn_out, w_ssm_out)
    return _out_mlp(h, merged, w_out.astype(BF16), mlp_norm_w, w_up.astype(BF16), w_down.astype(BF16))


def kernel(x, mix_norm_w, w_in, q_norm_w, k_norm_w, conv_w, conv_b, dt_bias, a_log, d_skip, ssm_norm_w,
           w_attn_out, w_ssm_out, w_out, mlp_norm_w, w_up, w_down):
    bsz, seq, d = x.shape
    slopes = _slope_table()
    h = x.reshape(bsz * seq, d)
    for layer in range(w_in.shape[0]):
        h = _layer(h, bsz, seq, mix_norm_w[layer], w_in[layer], q_norm_w[layer], k_norm_w[layer], conv_w[layer],
                   conv_b[layer], dt_bias[layer], a_log[layer], d_skip[layer], ssm_norm_w[layer], w_attn_out[layer],
                   w_ssm_out[layer], w_out[layer], mlp_norm_w[layer], w_up[layer], w_down[layer], slopes)
    return h.reshape(bsz, seq, d)
```

```python
import functools

import jax
import jax.numpy as jnp
from jax import lax
from jax.experimental import pallas as pl
from jax.experimental.pallas import tpu as pltpu

D_MODEL = 2048
ATTN_HEAD_DIM = 128
ATTN_HEADS = D_MODEL // ATTN_HEAD_DIM
ATTN_WIDTH = ATTN_HEADS * ATTN_HEAD_DIM
MOBA_BLOCK = 256
MOBA_TOPK = 3
SSM_INNER = 2 * D_MODEL
SSM_HEAD_DIM = 64
SSM_HEADS = SSM_INNER // SSM_HEAD_DIM
SSM_GROUPS = 8
SSM_STATE = 128
SSM_CONV = 4
SSM_CHUNK = 128
SSM_BC = SSM_GROUPS * SSM_STATE
SSM_CONV_DIM = SSM_INNER + 2 * SSM_BC
GROUP_WIDTH = SSM_INNER // SSM_GROUPS
D_FF = 4 * D_MODEL
EPS = 1e-6
NEG_INF = -1e30
LOG2E = 1.4426950408889634
Q_SCALE = ATTN_HEAD_DIM ** -0.5 * LOG2E
SLOPE_TERMS = 3
AUG_COLS = 16
SUM_ROWS = 16
BF16_SUBLANES = 16
SSD_CHUNKS_PER_STEP = 4

LANES = 128
CONV_HALO = 8
PROJ_TM = 1024
PROJ_TN = 1024
CONV_SUB = 256
N_GATES = 2 * D_MODEL
GATE_SKEW = SSM_HEADS
VMEM_LIMIT = 56 * 1024 * 1024

F32 = jnp.float32
BF16 = jnp.bfloat16

_NT = (((1,), (1,)), ((), ()))


def _params(*sem):
    return pltpu.CompilerParams(dimension_semantics=sem, vmem_limit_bytes=VMEM_LIMIT)


def _dot(a, b):
    return jnp.dot(a, b, preferred_element_type=F32)


def _dot_nt(a, b):
    return lax.dot_general(a, b, _NT, preferred_element_type=F32)


def _rmsnorm_dt_kernel(x_ref, w_ref, wdt_ref, wdtt_ref, o_ref, dt_ref, dtt_ref):
    x = x_ref[...]
    ms = jnp.mean(x * x, axis=-1, keepdims=True)
    hn = (x * lax.rsqrt(ms + EPS) * w_ref[...]).astype(o_ref.dtype)
    o_ref[...] = hn
    dt_ref[...] = _dot(hn, wdt_ref[...])
    dtt_ref[...] = _dot_nt(wdtt_ref[...], hn)


def _rmsnorm_dt(x, w, wdt, wdt_t, tm=512):
    t, d = x.shape
    n = wdt.shape[1]
    return pl.pallas_call(
        _rmsnorm_dt_kernel,
        grid=(t // tm,),
        in_specs=[pl.BlockSpec((tm, d), lambda i: (i, 0)), pl.BlockSpec((1, d), lambda i: (0, 0)),
                  pl.BlockSpec((d, n), lambda i: (0, 0)), pl.BlockSpec((n, d), lambda i: (0, 0))],
        out_specs=[pl.BlockSpec((tm, d), lambda i: (i, 0)), pl.BlockSpec((tm, n), lambda i: (i, 0)),
                   pl.BlockSpec((n, tm), lambda i: (0, i))],
        out_shape=[jax.ShapeDtypeStruct((t, d), BF16), jax.ShapeDtypeStruct((t, n), F32),
                   jax.ShapeDtypeStruct((n, t), F32)],
        compiler_params=_params("parallel"),
        name="rmsnorm_dt",
    )(x, w.reshape(1, d), wdt, wdt_t)


def _stage_weight(w_ref, wbf_ref, transpose=True):
    @pl.when(pl.program_id(1) == 0)
    def _():
        w = w_ref[...]
        wbf_ref[...] = (w.T if transpose else w).astype(BF16)


def _silu(x):
    half = 0.5 * x
    return half + half * jnp.tanh(half)


def _proj_act_kernel(a_ref, w_ref, o_ref, wbf_ref, *, act):
    _stage_weight(w_ref, wbf_ref)
    o_ref[...] = act(_dot(a_ref[...], wbf_ref[...])).astype(o_ref.dtype)


def _proj_gate_kernel(a_ref, w_ref, wnext_ref, o_ref, wbf_ref):
    @pl.when(pl.program_id(1) == 0)
    def _():
        w = jnp.concatenate([w_ref[GATE_SKEW:, :], wnext_ref[...]], axis=0)
        wbf_ref[...] = w.T.astype(BF16)

    o_ref[...] = (0.5 + 0.5 * jnp.tanh(0.5 * _dot(a_ref[...], wbf_ref[...]))).astype(o_ref.dtype)


def _proj_t_kernel(a_ref, wt_ref, o_ref, wbf_ref):
    _stage_weight(wt_ref, wbf_ref, transpose=False)
    o_ref[...] = _dot_nt(wbf_ref[...], a_ref[...]).astype(o_ref.dtype)


def _proj_knorm_kernel(a_ref, w_ref, nw_ref, o_ref, wbf_ref):
    _stage_weight(w_ref, wbf_ref)
    acc = _dot(a_ref[...], wbf_ref[...])
    nw = nw_ref[...]
    for h in range(acc.shape[1] // ATTN_HEAD_DIM):
        slab = acc[:, h * ATTN_HEAD_DIM:(h + 1) * ATTN_HEAD_DIM]
        ms = jnp.mean(slab * slab, axis=-1, keepdims=True)
        o_ref[:, h * ATTN_HEAD_DIM:(h + 1) * ATTN_HEAD_DIM] = (slab * lax.rsqrt(ms + EPS) * nw).astype(o_ref.dtype)


def _proj_qnorm_t_kernel(a_ref, wt_ref, nw_ref, o_ref, wbf_ref):
    _stage_weight(wt_ref, wbf_ref, transpose=False)
    acc = _dot_nt(wbf_ref[...], a_ref[...])
    nw = nw_ref[...]
    for h in range(acc.shape[0] // ATTN_HEAD_DIM):
        slab = acc[h * ATTN_HEAD_DIM:(h + 1) * ATTN_HEAD_DIM, :]
        ms = jnp.mean(slab * slab, axis=0, keepdims=True)
        qn = slab * lax.rsqrt(ms + EPS) * nw
        o_ref[h * ATTN_HEAD_DIM:(h + 1) * ATTN_HEAD_DIM, :] = (qn * Q_SCALE).astype(o_ref.dtype)


def _proj_conv_kernel(a_ref, w_ref, cw_ref, cb_ref, o_ref, wbf_ref, tail_ref, *, seq, sub):
    _stage_weight(w_ref, wbf_ref)
    i = pl.program_id(1)
    tm = a_ref.shape[0]
    w = wbf_ref[...]

    @pl.when((i * tm) % seq == 0)
    def _():
        tail_ref[...] = jnp.zeros_like(tail_ref)

    tail = tail_ref[...]
    for r in range(0, tm, sub):
        cur = _dot(a_ref[r:r + sub, :], w)
        ext = jnp.concatenate([tail, cur], axis=0)
        acc = cb_ref[...] + cw_ref[SSM_CONV - 1:SSM_CONV, :] * cur
        for back in range(1, SSM_CONV):
            shifted = pltpu.roll(ext, back, axis=0)[CONV_HALO:CONV_HALO + sub, :]
            acc = acc + cw_ref[SSM_CONV - 1 - back:SSM_CONV - back, :] * shifted
        o_ref[r:r + sub, :] = _silu(acc).astype(o_ref.dtype)
        tail = cur[sub - CONV_HALO:sub, :]
    tail_ref[...] = tail


def _project(a, w, body, *, n, col0=0, transposed=False, extra=(), extra_specs=(), scratch=(),
             out_dtype=BF16, tm=PROJ_TM, tn=PROJ_TN, name="proj"):
    t, k = a.shape
    assert n % tn == 0 and col0 % tn == 0 and t % tm == 0
    c0 = col0 // tn
    in_specs = [pl.BlockSpec((tm, k), lambda j, i: (i, 0)), pl.BlockSpec((tn, k), lambda j, i: (c0 + j, 0))]
    args = [a, w]
    if transposed:
        o_spec = pl.BlockSpec((tn, tm), lambda j, i: (j, i))
        o_shape = (n, t)
        w_block = (tn, k)
    else:
        o_spec = pl.BlockSpec((tm, tn), lambda j, i: (i, j))
        o_shape = (t, n)
        w_block = (k, tn)
    return pl.pallas_call(
        body,
        grid=(n // tn, t // tm),
        in_specs=in_specs + list(extra_specs),
        out_specs=o_spec,
        out_shape=jax.ShapeDtypeStruct(o_shape, out_dtype),
        scratch_shapes=[pltpu.VMEM(w_block, BF16)] + list(scratch),
        compiler_params=_params("parallel", "arbitrary"),
        name=name,
    )(*args, *extra)


def _moba_kernel(*refs, n_cast):
    slopes_ref, qt_ref, k_ref, vt_ref = refs[:4]
    cast_in = refs[4:4 + n_cast]
    o_ref = refs[4 + n_cast]
    cast_out = refs[5 + n_cast:5 + 2 * n_cast]
    (kaug_ref, qaug_ref, kmh_ref, kml_ref, sel_ref, acc_ref, sd_ref, s0_ref, s1_ref,
     vaug_ref) = refs[5 + 2 * n_cast:]
    for src_ref, dst_ref in zip(cast_in, cast_out):
        dst_ref[...] = src_ref[...].astype(dst_ref.dtype)

    hg = pl.program_id(1)
    i = pl.program_id(2)
    blk = MOBA_BLOCK
    dh = ATTN_HEAD_DIM
    seq = k_ref.shape[0]
    nb = seq // blk
    n_chain = qt_ref.shape[0] // dh
    n_split = SLOPE_TERMS

    @pl.when(i == 0)
    def _():
        r = lax.broadcasted_iota(jnp.int32, (nb, seq), 0)
        c = lax.broadcasted_iota(jnp.int32, (nb, seq), 1)
        lo = r * blk
        pool = jnp.where((c >= lo) & (c < lo + blk), 1.0 / blk, 0.0).astype(BF16)
        k_lane = lax.broadcasted_iota(jnp.int32, (blk, AUG_COLS), 1)
        k_row = lax.broadcasted_iota(jnp.int32, (blk, AUG_COLS), 0).astype(F32)
        q_row = lax.broadcasted_iota(jnp.int32, (AUG_COLS, blk), 0)
        q_lane = lax.broadcasted_iota(jnp.int32, (AUG_COLS, blk), 1).astype(F32)
        ones_row = lax.broadcasted_iota(jnp.int32, (SUM_ROWS, seq), 0) == 0
        for g in range(n_chain):
            vaug_ref[g, 0:dh, :] = vt_ref[g * dh:(g + 1) * dh, :]
            vaug_ref[g, dh:dh + SUM_ROWS, :] = jnp.where(ones_row, 1.0, 0.0).astype(BF16)
            kg = k_ref[:, g * dh:(g + 1) * dh]
            kmean = _dot(pool, kg)
            hi = kmean.astype(BF16)
            kmh_ref[g] = hi
            kml_ref[g] = (kmean - hi.astype(F32)).astype(BF16)
            kaug_ref[g, :, 0:dh] = kg
            k_aug = jnp.where((k_lane >= n_split) & (k_lane < 2 * n_split), k_row, 0.0)
            q_aug = jnp.where(q_row < n_split, -q_lane, 0.0)
            for t in range(n_split):
                s_t = slopes_ref[(hg * n_chain + g) * n_split + t]
                k_aug = jnp.where(k_lane == t, s_t, k_aug)
                q_aug = jnp.where(q_row == n_split + t, s_t, q_aug)
            k_aug = k_aug.astype(BF16)
            for jb in range(nb):
                kaug_ref[g, jb * blk:(jb + 1) * blk, dh:dh + AUG_COLS] = k_aug
            qaug_ref[g, dh:dh + AUG_COLS, :] = q_aug.astype(BF16)

    chains = range(n_chain)

    def score_stage(j, buf_ref):
        jc = jnp.minimum(j, i)
        for g in chains:
            kb = kaug_ref[g, pl.ds(pl.multiple_of(jc * blk, blk), blk), :]
            buf_ref[g] = _dot(kb, qaug_ref[g])

    def values(g, j):
        jc = jnp.minimum(j, i)
        return vaug_ref[g, :, pl.ds(pl.multiple_of(jc * blk, blk), blk)]

    def softmax_stage(j, buf_ref, ms):
        far = ((i - j) * blk).astype(F32)
        ms_new = []
        for g in chains:
            m = ms[g]
            c_shift = slopes_ref[(ATTN_HEADS + hg * n_chain + g) * n_split] * far
            keep = sel_ref[g, pl.ds(jnp.minimum(j, nb - 1), 1), :] > 0.0
            m_new = jnp.maximum(m, jnp.where(keep, jnp.max(buf_ref[g], axis=0, keepdims=True) - c_shift, NEG_INF))
            alpha = jnp.exp2(m - m_new)
            p = jnp.exp2(buf_ref[g] - jnp.where(keep, m_new + c_shift, -NEG_INF))
            ms_new.append(m_new)
            acc_ref[g] = alpha * acc_ref[g] + _dot(values(g, j), p.astype(BF16))
        return tuple(ms_new)

    for g in chains:
        qaug_ref[g, 0:dh, :] = qt_ref[g * dh:(g + 1) * dh, :]
    score_stage(i, sd_ref)
    score_stage(0, s0_ref)

    jidx = lax.broadcasted_iota(jnp.int32, (nb, blk), 0)
    for g in chains:
        qt = qt_ref[g * dh:(g + 1) * dh, :]
        gate = _dot(kmh_ref[g], qt) + _dot(kml_ref[g], qt)
        cnt = jnp.zeros((nb, blk), F32)
        for jp in range(nb):
            row = gate[jp:jp + 1, :]
            beats = (row > gate) | ((row == gate) & (jidx > jp))
            cnt = cnt + jnp.where(beats, (jp < i).astype(F32), 0.0)
        sel_ref[g] = jnp.where((cnt < MOBA_TOPK) & (jidx < i), 1.0, 0.0)

    kl = lax.broadcasted_iota(jnp.int32, (blk, blk), 0)
    ql = lax.broadcasted_iota(jnp.int32, (blk, blk), 1)
    ms = []
    for g in chains:
        s = jnp.where(kl <= ql, sd_ref[g], NEG_INF)
        m0 = jnp.max(s, axis=0, keepdims=True)
        p = jnp.exp2(s - m0)
        ms.append(m0)
        acc_ref[g] = _dot(values(g, i), p.astype(BF16))

    def body(t, ms):
        j = 2 * t
        score_stage(j + 1, s1_ref)
        ms = softmax_stage(j, s0_ref, ms)
        score_stage(j + 2, s0_ref)
        return softmax_stage(j + 1, s1_ref, ms)

    ms = lax.fori_loop(0, i // 2, body, tuple(ms))

    @pl.when(i % 2 == 1)
    def _():
        softmax_stage(i - 1, s0_ref, ms)

    for g in chains:
        acc = acc_ref[g]
        o_ref[:, g * dh:(g + 1) * dh] = (acc[0:dh] / acc[dh:dh + 1]).T.astype(o_ref.dtype)


def _moba_attention(qt, k, vt, slope_table, bsz, seq, to_round=(), heads_per_step=4):
    assert seq % MOBA_BLOCK == 0 and ATTN_HEADS % heads_per_step == 0
    nq = seq // MOBA_BLOCK
    nb = nq
    dh = ATTN_HEAD_DIM
    g = heads_per_step
    n_hg = ATTN_HEADS // g
    n_steps = bsz * n_hg * nq
    slab_specs = []
    for w in to_round:
        rows = w.shape[0] // n_steps
        assert rows * n_steps == w.shape[0] and rows % BF16_SUBLANES == 0
        slab_specs.append(pl.BlockSpec((rows, w.shape[1]), lambda b, h, i: ((b * n_hg + h) * nq + i, 0)))
    return pl.pallas_call(
        functools.partial(_moba_kernel, n_cast=len(to_round)),
        grid=(bsz, n_hg, nq),
        in_specs=[pl.BlockSpec(memory_space=pltpu.SMEM),
                  pl.BlockSpec((g * dh, MOBA_BLOCK), lambda b, h, i: (h, b * nq + i)),
                  pl.BlockSpec((seq, g * dh), lambda b, h, i: (b, h)),
                  pl.BlockSpec((g * dh, seq), lambda b, h, i: (h, b))] + slab_specs,
        out_specs=[pl.BlockSpec((MOBA_BLOCK, g * dh), lambda b, h, i: (b * nq + i, h))] + slab_specs,
        out_shape=[jax.ShapeDtypeStruct((bsz * seq, ATTN_WIDTH), BF16)]
                  + [jax.ShapeDtypeStruct(w.shape, BF16) for w in to_round],
        scratch_shapes=[pltpu.VMEM((g, seq, dh + AUG_COLS), BF16),
                        pltpu.VMEM((g, dh + AUG_COLS, MOBA_BLOCK), BF16),
                        pltpu.VMEM((g, nb, dh), BF16), pltpu.VMEM((g, nb, dh), BF16),
                        pltpu.VMEM((g, nb, MOBA_BLOCK), F32),
                        pltpu.VMEM((g, dh + SUM_ROWS, MOBA_BLOCK), F32)]
                       + [pltpu.VMEM((g, MOBA_BLOCK, MOBA_BLOCK), F32)] * 3
                       + [pltpu.VMEM((g, dh + SUM_ROWS, seq), BF16)],
        compiler_params=_params("parallel", "parallel", "arbitrary"),
        name="moba_attention",
    )(slope_table, qt, k, vt, *to_round)


def _slope_table():
    slopes = jnp.exp2(-8.0 * jnp.arange(1, ATTN_HEADS + 1, dtype=F32) / ATTN_HEADS) * LOG2E
    terms, rest = [], slopes
    for _ in range(SLOPE_TERMS):
        t = rest.astype(BF16).astype(F32)
        terms.append(t)
        rest = rest - t
    split = jnp.stack(terms, axis=1).reshape(-1)
    full = jnp.stack([slopes] * SLOPE_TERMS, axis=1).reshape(-1)
    return jnp.concatenate([split, full])


def _split3(x):
    h1 = x.astype(BF16)
    r1 = x - h1.astype(F32)
    h2 = r1.astype(BF16)
    h3 = (r1 - h2.astype(F32)).astype(BF16)
    return h1, h2, h3


def _softplus(x):
    return jnp.maximum(x, 0.0) + jnp.log1p(jnp.exp(-jnp.abs(x)))


def _ssd_kernel(act_ref, z_ref, dt_ref, dtt_ref, dtb_ref, dtbt_ref, alog_ref, alogt_ref,
                dskip_ref, normw_ref, o_ref, state_ref, y_ref):
    c = pl.program_id(1)
    ln = SSM_CHUNK

    @pl.when(c == 0)
    def _():
        state_ref[...] = jnp.zeros_like(state_ref)

    row = lax.broadcasted_iota(jnp.int32, (ln, ln), 0)
    col = lax.broadcasted_iota(jnp.int32, (ln, ln), 1)
    causal = row >= col
    tri = jnp.where(causal, 1.0, 0.0).astype(BF16)
    trit = jnp.where(row <= col, 1.0, 0.0).astype(BF16)
    left = lax.broadcasted_iota(jnp.int32, (1, LANES), 1) < SSM_HEAD_DIM
    left_full = lax.broadcasted_iota(jnp.int32, (ln, LANES), 1) < SSM_HEAD_DIM
    pairs_per_group = GROUP_WIDTH // LANES

    for cc in range(act_ref.shape[0] // ln):
        rows = slice(cc * ln, (cc + 1) * ln)
        dt = _softplus(dt_ref[rows, :] + dtb_ref[...])
        dtt = _softplus(dtt_ref[:, rows] + dtbt_ref[...])
        la = dt * (-jnp.exp(alog_ref[...]))
        lat = dtt * (-jnp.exp(alogt_ref[...]))
        a1, a2, a3 = _split3(la)
        acum = (_dot(tri, a1) + _dot(tri, a2) + _dot(tri, a3)) * LOG2E
        b1, b2, b3 = _split3(lat)
        acumt = (_dot(b1, trit) + _dot(b2, trit) + _dot(b3, trit)) * LOG2E
        rowdt = acumt - jnp.log2(dtt)

        for g in range(SSM_GROUPS):
            b0 = SSM_INNER + g * SSM_STATE
            c0 = SSM_INNER + SSM_BC + g * SSM_STATE
            bgb = act_ref[rows, b0:b0 + SSM_STATE]
            cgb = act_ref[rows, c0:c0 + SSM_STATE]
            cb = _dot_nt(cgb, bgb)
            bgt = bgb.astype(F32).T
            for pp in range(pairs_per_group):
                p = g * pairs_per_group + pp
                x2 = act_ref[rows, p * LANES:(p + 1) * LANES]
                x_heads = (jnp.where(left_full, x2, 0), jnp.where(left_full, 0, x2))
                ops = []
                eacs = []
                for hh in (2 * p, 2 * p + 1):
                    lhs = []
                    colb = jnp.broadcast_to(acum[:, hh:hh + 1], (ln, ln))
                    rowb = rowdt[hh:hh + 1, :]
                    decay_dt = jnp.exp2(jnp.where(causal, colb - rowb, -jnp.inf))
                    lhs.append((cb * decay_dt).astype(BF16))
                    wt = jnp.exp2(acumt[hh:hh + 1, ln - 1:ln] - rowb)
                    lhs.append((bgt * wt).astype(BF16))
                    eacs.append(jnp.exp2(colb))
                    ops.append(jnp.concatenate(lhs, axis=0))
                res = _dot(ops[0], x_heads[0]) + _dot(ops[1], x_heads[1])
                eac = jnp.where(left, eacs[0], eacs[1])
                st = state_ref[p]
                y = res[0:ln] + _dot(cgb, st.astype(BF16)) * eac
                y_ref[:, p * LANES:(p + 1) * LANES] = y
                state_ref[p] = st * eac[ln - 1:ln, :] + res[ln:2 * ln]
            g0 = g * GROUP_WIDTH
            xg = act_ref[rows, g0:g0 + GROUP_WIDTH].astype(F32)
            yg = y_ref[:, g0:g0 + GROUP_WIDTH] + xg * dskip_ref[:, g0:g0 + GROUP_WIDTH]
            yg = yg * z_ref[rows, g0:g0 + GROUP_WIDTH].astype(F32)
            ms = jnp.mean(yg * yg, axis=-1, keepdims=True)
            o_ref[rows, g0:g0 + GROUP_WIDTH] = (yg * lax.rsqrt(ms + EPS)
                                                * normw_ref[:, g0:g0 + GROUP_WIDTH]).astype(o_ref.dtype)


def _ssd_branch(act, zact, dt, dtt, dt_bias, a_log, d_skip, norm_w, bsz, seq):
    ln = SSM_CHUNK * SSD_CHUNKS_PER_STEP
    assert seq % ln == 0
    nc = seq // ln
    hs = SSM_HEADS
    row = lambda b, c: (b * nc + c, 0)
    fixed = lambda b, c: (0, 0)
    return pl.pallas_call(
        _ssd_kernel,
        grid=(bsz, nc),
        in_specs=[pl.BlockSpec((ln, SSM_CONV_DIM), row),
                  pl.BlockSpec((ln, SSM_INNER), row),
                  pl.BlockSpec((ln, hs), row),
                  pl.BlockSpec((hs, ln), lambda b, c: (0, b * nc + c)),
                  pl.BlockSpec((1, hs), fixed),
                  pl.BlockSpec((hs, 1), fixed),
                  pl.BlockSpec((1, hs), fixed),
                  pl.BlockSpec((hs, 1), fixed),
                  pl.BlockSpec((1, SSM_INNER), fixed),
                  pl.BlockSpec((1, SSM_INNER), fixed)],
        out_specs=pl.BlockSpec((ln, SSM_INNER), row),
        out_shape=jax.ShapeDtypeStruct((bsz * seq, SSM_INNER), BF16),
        scratch_shapes=[pltpu.VMEM((SSM_HEADS // 2, SSM_STATE, LANES), F32),
                        pltpu.VMEM((SSM_CHUNK, SSM_INNER), F32)],
        compiler_params=_params("arbitrary", "arbitrary"),
        name="ssd",
    )(act, zact, dt, dtt, dt_bias.reshape(1, hs), dt_bias.reshape(hs, 1),
      a_log.reshape(1, hs), a_log.reshape(hs, 1),
      jnp.repeat(d_skip, SSM_HEAD_DIM).reshape(1, SSM_INNER), norm_w.reshape(1, SSM_INNER))


def _merge_kernel(attn_ref, ssm_ref, ga_ref, gs_ref, wa_ref, ws_ref, o_ref):
    ya = _dot(attn_ref[...], wa_ref[...])
    ys = _dot(ssm_ref[...], ws_ref[...])
    o_ref[...] = (ga_ref[...].astype(F32) * ya + gs_ref[...].astype(F32) * ys).astype(o_ref.dtype)


def _merge(attn, ssm, gate, wa, ws, tm=1024, tn=512):
    t = attn.shape[0]
    n = wa.shape[1]
    nj = n // tn
    return pl.pallas_call(
        _merge_kernel,
        grid=(t // tm, nj),
        in_specs=[pl.BlockSpec((tm, attn.shape[1]), lambda i, j: (i, 0)),
                  pl.BlockSpec((tm, ssm.shape[1]), lambda i, j: (i, 0)),
                  pl.BlockSpec((tm, tn), lambda i, j: (i, j)),
                  pl.BlockSpec((tm, tn), lambda i, j: (i, j + nj)),
                  pl.BlockSpec((wa.shape[0], tn), lambda i, j: (0, j)),
                  pl.BlockSpec((ws.shape[0], tn), lambda i, j: (0, j))],
        out_specs=pl.BlockSpec((tm, tn), lambda i, j: (i, j)),
        out_shape=jax.ShapeDtypeStruct((t, n), BF16),
        compiler_params=_params("parallel", "parallel"),
        name="merge",
    )(attn, ssm, gate, gate, wa, ws)


def _out_mlp_kernel(x_ref, m_ref, wo_ref, nw_ref, wu_ref, wd_ref, o_ref, hn_ref):
    j = pl.program_id(1)

    @pl.when(j == 0)
    def _():
        h = x_ref[...] + _dot(m_ref[...], wo_ref[...])
        ms = jnp.mean(h * h, axis=-1, keepdims=True)
        hn_ref[...] = (h * lax.rsqrt(ms + EPS) * nw_ref[...]).astype(hn_ref.dtype)
        o_ref[...] = h

    u = _dot(hn_ref[...], wu_ref[...])
    act = jnp.square(jnp.maximum(u, 0.0)).astype(BF16)
    o_ref[...] += _dot(act, wd_ref[...])


def _out_mlp(x, merged, w_out, norm_w, w_up, w_down, tm=512, tf=1024):
    t, d = x.shape
    f = w_up.shape[1]
    return pl.pallas_call(
        _out_mlp_kernel,
        grid=(t // tm, f // tf),
        in_specs=[pl.BlockSpec((tm, d), lambda i, j: (i, 0)),
                  pl.BlockSpec((tm, merged.shape[1]), lambda i, j: (i, 0)),
                  pl.BlockSpec(w_out.shape, lambda i, j: (0, 0), pipeline_mode=pl.Buffered(1)),
                  pl.BlockSpec((1, d), lambda i, j: (0, 0)),
                  pl.BlockSpec((d, tf), lambda i, j: (0, j)),
                  pl.BlockSpec((tf, d), lambda i, j: (j, 0))],
        out_specs=pl.BlockSpec((tm, d), lambda i, j: (i, 0)),
        out_shape=jax.ShapeDtypeStruct((t, d), F32),
        scratch_shapes=[pltpu.VMEM((tm, d), BF16)],
        compiler_params=_params("parallel", "arbitrary"),
        name="out_mlp",
    )(x, merged, w_out, norm_w.reshape(1, d), w_up, w_down)


def _layer(h, bsz, seq, mix_norm_w, w_in, q_norm_w, k_norm_w, conv_w, conv_b, dt_bias, a_log, d_skip,
           ssm_norm_w, w_attn_out, w_ssm_out, w_out, mlp_norm_w, w_up, w_down, slopes):
    o_k = ATTN_WIDTH
    o_v = 2 * ATTN_WIDTH
    o_z = 3 * ATTN_WIDTH
    o_x = o_z + SSM_INNER
    o_dt = o_x + SSM_CONV_DIM
    o_g = o_dt + SSM_HEADS
    assert o_g - o_dt == GATE_SKEW and o_dt % PROJ_TN == 0
    w_t = jnp.swapaxes(w_in, 0, 1)
    wdt_t = w_t[o_dt:o_g].astype(BF16)
    head_col = pl.BlockSpec((ATTN_HEAD_DIM, 1), lambda j, i: (0, 0))
    head_row = pl.BlockSpec((1, ATTN_HEAD_DIM), lambda j, i: (0, 0))
    tn = PROJ_TN
    skew_blocks = tn // GATE_SKEW

    hn, dt, dtt = _rmsnorm_dt(h, mix_norm_w, wdt_t.T, wdt_t)
    qt = _project(hn, w_t, _proj_qnorm_t_kernel, n=ATTN_WIDTH, transposed=True,
                  extra=(q_norm_w.reshape(ATTN_HEAD_DIM, 1),), extra_specs=(head_col,), name="proj_q")
    k = _project(hn, w_t, _proj_knorm_kernel, n=ATTN_WIDTH, col0=o_k,
                 extra=(k_norm_w.reshape(1, ATTN_HEAD_DIM),), extra_specs=(head_row,), name="proj_k")
    vt = _project(hn, w_t, _proj_t_kernel, n=ATTN_WIDTH, col0=o_v, transposed=True, name="proj_v")
    zact = _project(hn, w_t, functools.partial(_proj_act_kernel, act=_silu), n=SSM_INNER, col0=o_z, name="proj_z")
    act = _project(hn, w_t, functools.partial(_proj_conv_kernel, seq=seq, sub=CONV_SUB), n=SSM_CONV_DIM, col0=o_x,
                   extra=(conv_w, conv_b.reshape(1, -1)),
                   extra_specs=(pl.BlockSpec((SSM_CONV, tn), lambda j, i: (0, j)),
                                pl.BlockSpec((1, tn), lambda j, i: (0, j))),
                   scratch=(pltpu.VMEM((CONV_HALO, tn), F32),), name="proj_xbc")
    gates = _project(hn, w_t, _proj_gate_kernel, n=N_GATES, col0=o_dt, extra=(w_t,),
                     extra_specs=(pl.BlockSpec((GATE_SKEW, w_t.shape[1]),
                                               lambda j, i: (o_dt // GATE_SKEW + (j + 1) * skew_blocks, 0)),),
                     name="proj_gate")

    attn, wa_bf, ws_bf, wo_bf, wu_bf, wd_bf = _moba_attention(
        qt, k, vt, slopes, bsz, seq, to_round=(w_attn_out, w_ssm_out, w_out, w_up, w_down))
    ssm = _ssd_branch(act, zact, dt, dtt, dt_bias, a_log, d_skip, ssm_norm_w, bsz, seq)
    merged = _merge(attn, ssm, gates, wa_bf, ws_bf)
    return _out_mlp(h, merged, wo_bf, mlp_norm_w, wu_bf, wd_bf)


def kernel(x, mix_norm_w, w_in, q_norm_w, k_norm_w, conv_w, conv_b, dt_bias, a_log, d_skip, ssm_norm_w,
           w_attn_out, w_ssm_out, w_out, mlp_norm_w, w_up, w_down):
    bsz, seq, d = x.shape
    slopes = _slope_table()
    h = x.reshape(bsz * seq, d)
    for layer in range(w_in.shape[0]):
        h = _layer(h, bsz, seq, mix_norm_w[layer], w_in[layer], q_norm_w[layer], k_norm_w[layer], conv_w[layer],
                   conv_b[layer], dt_bias[layer], a_log[layer], d_skip[layer], ssm_norm_w[layer], w_attn_out[layer],
                   w_ssm_out[layer], w_out[layer], mlp_norm_w[layer], w_up[layer], w_down[layer], slopes)
    return h.reshape(bsz, seq, d)
```

```python
import functools

import jax
import jax.numpy as jnp
from jax import lax
from jax.experimental import pallas as pl
from jax.experimental.pallas import tpu as pltpu

D_MODEL = 2048
ATTN_HEAD_DIM = 128
ATTN_HEADS = D_MODEL // ATTN_HEAD_DIM
ATTN_WIDTH = ATTN_HEADS * ATTN_HEAD_DIM
MOBA_BLOCK = 256
MOBA_TOPK = 3
SSM_INNER = 2 * D_MODEL
SSM_HEAD_DIM = 64
SSM_HEADS = SSM_INNER // SSM_HEAD_DIM
SSM_GROUPS = 8
SSM_STATE = 128
SSM_CONV = 4
SSM_CHUNK = 128
SSM_BC = SSM_GROUPS * SSM_STATE
SSM_CONV_DIM = SSM_INNER + 2 * SSM_BC
GROUP_WIDTH = SSM_INNER // SSM_GROUPS
D_FF = 4 * D_MODEL
EPS = 1e-6
NEG_INF = -1e30
LOG2E = 1.4426950408889634
Q_SCALE = ATTN_HEAD_DIM ** -0.5 * LOG2E
SLOPE_TERMS = 3
AUG_COLS = 16
SUM_ROWS = 16
BF16_SUBLANES = 16
SSD_CHUNKS_PER_STEP = 4

LANES = 128
CONV_HALO = 8
PROJ_TM = 1024
PROJ_TN = 1024
CONV_SUB = 256
N_GATES = 2 * D_MODEL
GATE_SKEW = SSM_HEADS
VMEM_LIMIT = 56 * 1024 * 1024

F32 = jnp.float32
BF16 = jnp.bfloat16

_NT = (((1,), (1,)), ((), ()))


def _params(*sem):
    return pltpu.CompilerParams(dimension_semantics=sem, vmem_limit_bytes=VMEM_LIMIT)


def _dot(a, b):
    return jnp.dot(a, b, preferred_element_type=F32)


def _dot_nt(a, b):
    return lax.dot_general(a, b, _NT, preferred_element_type=F32)


def _rmsnorm_dt_kernel(x_ref, w_ref, wdt_ref, wdtt_ref, o_ref, dt_ref, dtt_ref):
    x = x_ref[...]
    ms = jnp.mean(x * x, axis=-1, keepdims=True)
    hn = (x * lax.rsqrt(ms + EPS) * w_ref[...]).astype(o_ref.dtype)
    o_ref[...] = hn
    dt_ref[...] = _dot(hn, wdt_ref[...])
    dtt_ref[...] = _dot_nt(wdtt_ref[...], hn)


def _rmsnorm_dt(x, w, wdt, wdt_t, tm=512):
    t, d = x.shape
    n = wdt.shape[1]
    return pl.pallas_call(
        _rmsnorm_dt_kernel,
        grid=(t // tm,),
        in_specs=[pl.BlockSpec((tm, d), lambda i: (i, 0)), pl.BlockSpec((1, d), lambda i: (0, 0)),
                  pl.BlockSpec((d, n), lambda i: (0, 0)), pl.BlockSpec((n, d), lambda i: (0, 0))],
        out_specs=[pl.BlockSpec((tm, d), lambda i: (i, 0)), pl.BlockSpec((tm, n), lambda i: (i, 0)),
                   pl.BlockSpec((n, tm), lambda i: (0, i))],
        out_shape=[jax.ShapeDtypeStruct((t, d), BF16), jax.ShapeDtypeStruct((t, n), F32),
                   jax.ShapeDtypeStruct((n, t), F32)],
        compiler_params=_params("parallel"),
        name="rmsnorm_dt",
    )(x, w.reshape(1, d), wdt, wdt_t)


def _stage_weight(w_ref, wbf_ref, transpose=True):
    @pl.when(pl.program_id(1) == 0)
    def _():
        w = w_ref[...]
        wbf_ref[...] = (w.T if transpose else w).astype(BF16)


def _silu(x):
    half = 0.5 * x
    return half + half * jnp.tanh(half)


def _proj_act_kernel(a_ref, w_ref, o_ref, wbf_ref, *, act):
    _stage_weight(w_ref, wbf_ref)
    o_ref[...] = act(_dot(a_ref[...], wbf_ref[...])).astype(o_ref.dtype)


def _proj_gate_kernel(a_ref, w_ref, wnext_ref, o_ref, wbf_ref):
    @pl.when(pl.program_id(1) == 0)
    def _():
        w = jnp.concatenate([w_ref[GATE_SKEW:, :], wnext_ref[...]], axis=0)
        wbf_ref[...] = w.T.astype(BF16)

    o_ref[...] = (0.5 + 0.5 * jnp.tanh(0.5 * _dot(a_ref[...], wbf_ref[...]))).astype(o_ref.dtype)


def _proj_t_kernel(a_ref, wt_ref, o_ref, wbf_ref):
    _stage_weight(wt_ref, wbf_ref, transpose=False)
    o_ref[...] = _dot_nt(wbf_ref[...], a_ref[...]).astype(o_ref.dtype)


def _proj_knorm_kernel(a_ref, w_ref, nw_ref, o_ref, wbf_ref):
    _stage_weight(w_ref, wbf_ref)
    acc = _dot(a_ref[...], wbf_ref[...])
    nw = nw_ref[...]
    for h in range(acc.shape[1] // ATTN_HEAD_DIM):
        slab = acc[:, h * ATTN_HEAD_DIM:(h + 1) * ATTN_HEAD_DIM]
        ms = jnp.mean(slab * slab, axis=-1, keepdims=True)
        o_ref[:, h * ATTN_HEAD_DIM:(h + 1) * ATTN_HEAD_DIM] = (slab * lax.rsqrt(ms + EPS) * nw).astype(o_ref.dtype)


def _proj_qnorm_t_kernel(a_ref, wt_ref, nw_ref, o_ref, wbf_ref):
    _stage_weight(wt_ref, wbf_ref, transpose=False)
    acc = _dot_nt(wbf_ref[...], a_ref[...])
    nw = nw_ref[...]
    for h in range(acc.shape[0] // ATTN_HEAD_DIM):
        slab = acc[h * ATTN_HEAD_DIM:(h + 1) * ATTN_HEAD_DIM, :]
        ms = jnp.mean(slab * slab, axis=0, keepdims=True)
        qn = slab * lax.rsqrt(ms + EPS) * nw
        o_ref[h * ATTN_HEAD_DIM:(h + 1) * ATTN_HEAD_DIM, :] = (qn * Q_SCALE).astype(o_ref.dtype)


def _proj_conv_kernel(a_ref, w_ref, cw_ref, cb_ref, o_ref, wbf_ref, tail_ref, *, seq, sub):
    _stage_weight(w_ref, wbf_ref)
    i = pl.program_id(1)
    tm = a_ref.shape[0]
    w = wbf_ref[...]

    @pl.when((i * tm) % seq == 0)
    def _():
        tail_ref[...] = jnp.zeros_like(tail_ref)

    tail = tail_ref[...]
    for r in range(0, tm, sub):
        cur = _dot(a_ref[r:r + sub, :], w)
        ext = jnp.concatenate([tail, cur], axis=0)
        acc = cb_ref[...] + cw_ref[SSM_CONV - 1:SSM_CONV, :] * cur
        for back in range(1, SSM_CONV):
            shifted = pltpu.roll(ext, back, axis=0)[CONV_HALO:CONV_HALO + sub, :]
            acc = acc + cw_ref[SSM_CONV - 1 - back:SSM_CONV - back, :] * shifted
        o_ref[r:r + sub, :] = _silu(acc).astype(o_ref.dtype)
        tail = cur[sub - CONV_HALO:sub, :]
    tail_ref[...] = tail


def _rounding_too(body, n_in, n_round):
    def kernel(*refs):
        slabs_in = refs[n_in:n_in + n_round]
        o_ref = refs[n_in + n_round]
        slabs_out = refs[n_in + n_round + 1:n_in + 2 * n_round + 1]
        for src_ref, dst_ref in zip(slabs_in, slabs_out):
            dst_ref[...] = src_ref[...].astype(dst_ref.dtype)
        body(*refs[:n_in], o_ref, *refs[n_in + 2 * n_round + 1:])
    return kernel


def _project(a, w, body, *, n, col0=0, transposed=False, extra=(), extra_specs=(), scratch=(), to_round=(),
             out_dtype=BF16, tm=PROJ_TM, tn=PROJ_TN, name="proj"):
    t, k = a.shape
    assert n % tn == 0 and col0 % tn == 0 and t % tm == 0
    c0 = col0 // tn
    n_i = t // tm
    n_steps = (n // tn) * n_i
    in_specs = [pl.BlockSpec((tm, k), lambda j, i: (i, 0)), pl.BlockSpec((tn, k), lambda j, i: (c0 + j, 0))]
    args = [a, w]
    if transposed:
        o_spec = pl.BlockSpec((tn, tm), lambda j, i: (j, i))
        o_shape = (n, t)
        w_block = (tn, k)
    else:
        o_spec = pl.BlockSpec((tm, tn), lambda j, i: (i, j))
        o_shape = (t, n)
        w_block = (k, tn)
    slab_specs = []
    for r in to_round:
        rows = r.shape[0] // n_steps
        assert rows * n_steps == r.shape[0] and rows % BF16_SUBLANES == 0
        slab_specs.append(pl.BlockSpec((rows, r.shape[1]), lambda j, i: (j * n_i + i, 0)))
    out = pl.pallas_call(
        _rounding_too(body, 2 + len(extra), len(to_round)),
        grid=(n // tn, n_i),
        in_specs=in_specs + list(extra_specs) + slab_specs,
        out_specs=[o_spec] + slab_specs,
        out_shape=[jax.ShapeDtypeStruct(o_shape, out_dtype)] + [jax.ShapeDtypeStruct(r.shape, BF16) for r in to_round],
        scratch_shapes=[pltpu.VMEM(w_block, BF16)] + list(scratch),
        compiler_params=_params("parallel", "arbitrary"),
        name=name,
    )(*args, *extra, *to_round)
    return out if to_round else out[0]


def _moba_kernel(*refs, n_cast):
    slopes_ref, qt_ref, k_ref, vt_ref = refs[:4]
    cast_in = refs[4:4 + n_cast]
    o_ref = refs[4 + n_cast]
    cast_out = refs[5 + n_cast:5 + 2 * n_cast]
    (kaug_ref, qaug_ref, kmh_ref, kml_ref, sel_ref, acc_ref, sd_ref, s0_ref, s1_ref,
     vaug_ref) = refs[5 + 2 * n_cast:]
    for src_ref, dst_ref in zip(cast_in, cast_out):
        dst_ref[...] = src_ref[...].astype(dst_ref.dtype)

    hg = pl.program_id(1)
    i = pl.program_id(2)
    blk = MOBA_BLOCK
    dh = ATTN_HEAD_DIM
    seq = k_ref.shape[0]
    nb = seq // blk
    n_chain = qt_ref.shape[0] // dh
    n_split = SLOPE_TERMS

    @pl.when(i == 0)
    def _():
        r = lax.broadcasted_iota(jnp.int32, (nb, seq), 0)
        c = lax.broadcasted_iota(jnp.int32, (nb, seq), 1)
        lo = r * blk
        pool = jnp.where((c >= lo) & (c < lo + blk), 1.0 / blk, 0.0).astype(BF16)
        k_lane = lax.broadcasted_iota(jnp.int32, (blk, AUG_COLS), 1)
        k_row = lax.broadcasted_iota(jnp.int32, (blk, AUG_COLS), 0).astype(F32)
        q_row = lax.broadcasted_iota(jnp.int32, (AUG_COLS, blk), 0)
        q_lane = lax.broadcasted_iota(jnp.int32, (AUG_COLS, blk), 1).astype(F32)
        ones_row = lax.broadcasted_iota(jnp.int32, (SUM_ROWS, seq), 0) == 0
        for g in range(n_chain):
            vaug_ref[g, 0:dh, :] = vt_ref[g * dh:(g + 1) * dh, :]
            vaug_ref[g, dh:dh + SUM_ROWS, :] = jnp.where(ones_row, 1.0, 0.0).astype(BF16)
            kg = k_ref[:, g * dh:(g + 1) * dh]
            kmean = _dot(pool, kg)
            hi = kmean.astype(BF16)
            kmh_ref[g] = hi
            kml_ref[g] = (kmean - hi.astype(F32)).astype(BF16)
            kaug_ref[g, :, 0:dh] = kg
            k_aug = jnp.where((k_lane >= n_split) & (k_lane < 2 * n_split), k_row, 0.0)
            q_aug = jnp.where(q_row < n_split, -q_lane, 0.0)
            for t in range(n_split):
                s_t = slopes_ref[(hg * n_chain + g) * n_split + t]
                k_aug = jnp.where(k_lane == t, s_t, k_aug)
                q_aug = jnp.where(q_row == n_split + t, s_t, q_aug)
            k_aug = k_aug.astype(BF16)
            for jb in range(nb):
                kaug_ref[g, jb * blk:(jb + 1) * blk, dh:dh + AUG_COLS] = k_aug
            qaug_ref[g, dh:dh + AUG_COLS, :] = q_aug.astype(BF16)

    chains = range(n_chain)

    def score_stage(j, buf_ref):
        jc = jnp.minimum(j, i)
        for g in chains:
            kb = kaug_ref[g, pl.ds(pl.multiple_of(jc * blk, blk), blk), :]
            buf_ref[g] = _dot(kb, qaug_ref[g])

    def values(g, j):
        jc = jnp.minimum(j, i)
        return vaug_ref[g, :, pl.ds(pl.multiple_of(jc * blk, blk), blk)]

    def softmax_stage(j, buf_ref, ms):
        far = ((i - j) * blk).astype(F32)
        ms_new = []
        for g in chains:
            m = ms[g]
            c_shift = slopes_ref[(ATTN_HEADS + hg * n_chain + g) * n_split] * far
            keep = sel_ref[g, pl.ds(jnp.minimum(j, nb - 1), 1), :] > 0.0
            m_new = jnp.maximum(m, jnp.where(keep, jnp.max(buf_ref[g], axis=0, keepdims=True) - c_shift, NEG_INF))
            alpha = jnp.exp2(m - m_new)
            p = jnp.exp2(buf_ref[g] - jnp.where(keep, m_new + c_shift, -NEG_INF))
            ms_new.append(m_new)
            acc_ref[g] = alpha * acc_ref[g] + _dot(values(g, j), p.astype(BF16))
        return tuple(ms_new)

    for g in chains:
        qaug_ref[g, 0:dh, :] = qt_ref[g * dh:(g + 1) * dh, :]
    score_stage(i, sd_ref)
    score_stage(0, s0_ref)

    jidx = lax.broadcasted_iota(jnp.int32, (nb, blk), 0)
    for g in chains:
        qt = qt_ref[g * dh:(g + 1) * dh, :]
        gate = _dot(kmh_ref[g], qt) + _dot(kml_ref[g], qt)
        cnt = jnp.zeros((nb, blk), F32)
        for jp in range(nb):
            row = gate[jp:jp + 1, :]
            beats = (row > gate) | ((row == gate) & (jidx > jp))
            cnt = cnt + jnp.where(beats, (jp < i).astype(F32), 0.0)
        sel_ref[g] = jnp.where((cnt < MOBA_TOPK) & (jidx < i), 1.0, 0.0)

    kl = lax.broadcasted_iota(jnp.int32, (blk, blk), 0)
    ql = lax.broadcasted_iota(jnp.int32, (blk, blk), 1)
    ms = []
    for g in chains:
        s = jnp.where(kl <= ql, sd_ref[g], NEG_INF)
        m0 = jnp.max(s, axis=0, keepdims=True)
        p = jnp.exp2(s - m0)
        ms.append(m0)
        acc_ref[g] = _dot(values(g, i), p.astype(BF16))

    def body(t, ms):
        j = 2 * t
        score_stage(j + 1, s1_ref)
        ms = softmax_stage(j, s0_ref, ms)
        score_stage(j + 2, s0_ref)
        return softmax_stage(j + 1, s1_ref, ms)

    ms = lax.fori_loop(0, i // 2, body, tuple(ms))

    @pl.when(i % 2 == 1)
    def _():
        softmax_stage(i - 1, s0_ref, ms)

    for g in chains:
        acc = acc_ref[g]
        o_ref[:, g * dh:(g + 1) * dh] = (acc[0:dh] / acc[dh:dh + 1]).T.astype(o_ref.dtype)


def _moba_attention(qt, k, vt, slope_table, bsz, seq, to_round=(), heads_per_step=4):
    assert seq % MOBA_BLOCK == 0 and ATTN_HEADS % heads_per_step == 0
    nq = seq // MOBA_BLOCK
    nb = nq
    dh = ATTN_HEAD_DIM
    g = heads_per_step
    n_hg = ATTN_HEADS // g
    n_steps = bsz * n_hg * nq
    slab_specs = []
    for w in to_round:
        rows = w.shape[0] // n_steps
        assert rows * n_steps == w.shape[0] and rows % BF16_SUBLANES == 0
        slab_specs.append(pl.BlockSpec((rows, w.shape[1]), lambda b, h, i: ((b * n_hg + h) * nq + i, 0)))
    return pl.pallas_call(
        functools.partial(_moba_kernel, n_cast=len(to_round)),
        grid=(bsz, n_hg, nq),
        in_specs=[pl.BlockSpec(memory_space=pltpu.SMEM),
                  pl.BlockSpec((g * dh, MOBA_BLOCK), lambda b, h, i: (h, b * nq + i)),
                  pl.BlockSpec((seq, g * dh), lambda b, h, i: (b, h)),
                  pl.BlockSpec((g * dh, seq), lambda b, h, i: (h, b))] + slab_specs,
        out_specs=[pl.BlockSpec((MOBA_BLOCK, g * dh), lambda b, h, i: (b * nq + i, h))] + slab_specs,
        out_shape=[jax.ShapeDtypeStruct((bsz * seq, ATTN_WIDTH), BF16)]
                  + [jax.ShapeDtypeStruct(w.shape, BF16) for w in to_round],
        scratch_shapes=[pltpu.VMEM((g, seq, dh + AUG_COLS), BF16),
                        pltpu.VMEM((g, dh + AUG_COLS, MOBA_BLOCK), BF16),
                        pltpu.VMEM((g, nb, dh), BF16), pltpu.VMEM((g, nb, dh), BF16),
                        pltpu.VMEM((g, nb, MOBA_BLOCK), F32),
                        pltpu.VMEM((g, dh + SUM_ROWS, MOBA_BLOCK), F32)]
                       + [pltpu.VMEM((g, MOBA_BLOCK, MOBA_BLOCK), F32)] * 3
                       + [pltpu.VMEM((g, dh + SUM_ROWS, seq), BF16)],
        compiler_params=_params("parallel", "parallel", "arbitrary"),
        name="moba_attention",
    )(slope_table, qt, k, vt, *to_round)


def _slope_table():
    slopes = jnp.exp2(-8.0 * jnp.arange(1, ATTN_HEADS + 1, dtype=F32) / ATTN_HEADS) * LOG2E
    terms, rest = [], slopes
    for _ in range(SLOPE_TERMS):
        t = rest.astype(BF16).astype(F32)
        terms.append(t)
        rest = rest - t
    split = jnp.stack(terms, axis=1).reshape(-1)
    full = jnp.stack([slopes] * SLOPE_TERMS, axis=1).reshape(-1)
    return jnp.concatenate([split, full])


def _split3(x):
    h1 = x.astype(BF16)
    r1 = x - h1.astype(F32)
    h2 = r1.astype(BF16)
    h3 = (r1 - h2.astype(F32)).astype(BF16)
    return h1, h2, h3


def _softplus(x):
    return jnp.maximum(x, 0.0) + jnp.log1p(jnp.exp(-jnp.abs(x)))


def _ssd_kernel(act_ref, z_ref, dt_ref, dtt_ref, dtb_ref, dtbt_ref, alog_ref, alogt_ref,
                dskip_ref, normw_ref, o_ref, state_ref, y_ref):
    c = pl.program_id(1)
    ln = SSM_CHUNK

    @pl.when(c == 0)
    def _():
        state_ref[...] = jnp.zeros_like(state_ref)

    row = lax.broadcasted_iota(jnp.int32, (ln, ln), 0)
    col = lax.broadcasted_iota(jnp.int32, (ln, ln), 1)
    causal = row >= col
    tri = jnp.where(causal, 1.0, 0.0).astype(BF16)
    trit = jnp.where(row <= col, 1.0, 0.0).astype(BF16)
    left = lax.broadcasted_iota(jnp.int32, (1, LANES), 1) < SSM_HEAD_DIM
    left_full = lax.broadcasted_iota(jnp.int32, (ln, LANES), 1) < SSM_HEAD_DIM
    pairs_per_group = GROUP_WIDTH // LANES

    for cc in range(act_ref.shape[0] // ln):
        rows = slice(cc * ln, (cc + 1) * ln)
        dt = _softplus(dt_ref[rows, :] + dtb_ref[...])
        dtt = _softplus(dtt_ref[:, rows] + dtbt_ref[...])
        la = dt * (-jnp.exp(alog_ref[...]))
        lat = dtt * (-jnp.exp(alogt_ref[...]))
        a1, a2, a3 = _split3(la)
        acum = (_dot(tri, a1) + _dot(tri, a2) + _dot(tri, a3)) * LOG2E
        b1, b2, b3 = _split3(lat)
        acumt = (_dot(b1, trit) + _dot(b2, trit) + _dot(b3, trit)) * LOG2E
        rowdt = acumt - jnp.log2(dtt)

        for g in range(SSM_GROUPS):
            b0 = SSM_INNER + g * SSM_STATE
            c0 = SSM_INNER + SSM_BC + g * SSM_STATE
            bgb = act_ref[rows, b0:b0 + SSM_STATE]
            cgb = act_ref[rows, c0:c0 + SSM_STATE]
            cb = _dot_nt(cgb, bgb)
            bgt = bgb.astype(F32).T
            for pp in range(pairs_per_group):
                p = g * pairs_per_group + pp
                x2 = act_ref[rows, p * LANES:(p + 1) * LANES]
                x_heads = (jnp.where(left_full, x2, 0), jnp.where(left_full, 0, x2))
                ops = []
                eacs = []
                for hh in (2 * p, 2 * p + 1):
                    lhs = []
                    colb = jnp.broadcast_to(acum[:, hh:hh + 1], (ln, ln))
                    rowb = rowdt[hh:hh + 1, :]
                    decay_dt = jnp.exp2(jnp.where(causal, colb - rowb, -jnp.inf))
                    lhs.append((cb * decay_dt).astype(BF16))
                    wt = jnp.exp2(acumt[hh:hh + 1, ln - 1:ln] - rowb)
                    lhs.append((bgt * wt).astype(BF16))
                    eacs.append(jnp.exp2(colb))
                    ops.append(jnp.concatenate(lhs, axis=0))
                res = _dot(ops[0], x_heads[0]) + _dot(ops[1], x_heads[1])
                eac = jnp.where(left, eacs[0], eacs[1])
                st = state_ref[p]
                y = res[0:ln] + _dot(cgb, st.astype(BF16)) * eac
                y_ref[:, p * LANES:(p + 1) * LANES] = y
                state_ref[p] = st * eac[ln - 1:ln, :] + res[ln:2 * ln]
            g0 = g * GROUP_WIDTH
            xg = act_ref[rows, g0:g0 + GROUP_WIDTH].astype(F32)
            yg = y_ref[:, g0:g0 + GROUP_WIDTH] + xg * dskip_ref[:, g0:g0 + GROUP_WIDTH]
            yg = yg * z_ref[rows, g0:g0 + GROUP_WIDTH].astype(F32)
            ms = jnp.mean(yg * yg, axis=-1, keepdims=True)
            o_ref[rows, g0:g0 + GROUP_WIDTH] = (yg * lax.rsqrt(ms + EPS)
                                                * normw_ref[:, g0:g0 + GROUP_WIDTH]).astype(o_ref.dtype)


def _ssd_branch(act, zact, dt, dtt, dt_bias, a_log, d_skip, norm_w, bsz, seq):
    ln = SSM_CHUNK * SSD_CHUNKS_PER_STEP
    assert seq % ln == 0
    nc = seq // ln
    hs = SSM_HEADS
    row = lambda b, c: (b * nc + c, 0)
    fixed = lambda b, c: (0, 0)
    return pl.pallas_call(
        _ssd_kernel,
        grid=(bsz, nc),
        in_specs=[pl.BlockSpec((ln, SSM_CONV_DIM), row),
                  pl.BlockSpec((ln, SSM_INNER), row),
                  pl.BlockSpec((ln, hs), row),
                  pl.BlockSpec((hs, ln), lambda b, c: (0, b * nc + c)),
                  pl.BlockSpec((1, hs), fixed),
                  pl.BlockSpec((hs, 1), fixed),
                  pl.BlockSpec((1, hs), fixed),
                  pl.BlockSpec((hs, 1), fixed),
                  pl.BlockSpec((1, SSM_INNER), fixed),
                  pl.BlockSpec((1, SSM_INNER), fixed)],
        out_specs=pl.BlockSpec((ln, SSM_INNER), row),
        out_shape=jax.ShapeDtypeStruct((bsz * seq, SSM_INNER), BF16),
        scratch_shapes=[pltpu.VMEM((SSM_HEADS // 2, SSM_STATE, LANES), F32),
                        pltpu.VMEM((SSM_CHUNK, SSM_INNER), F32)],
        compiler_params=_params("arbitrary", "arbitrary"),
        name="ssd",
    )(act, zact, dt, dtt, dt_bias.reshape(1, hs), dt_bias.reshape(hs, 1),
      a_log.reshape(1, hs), a_log.reshape(hs, 1),
      jnp.repeat(d_skip, SSM_HEAD_DIM).reshape(1, SSM_INNER), norm_w.reshape(1, SSM_INNER))


def _merge_kernel(attn_ref, ssm_ref, ga_ref, gs_ref, wa_ref, ws_ref, o_ref):
    ya = _dot(attn_ref[...], wa_ref[...])
    ys = _dot(ssm_ref[...], ws_ref[...])
    o_ref[...] = (ga_ref[...].astype(F32) * ya + gs_ref[...].astype(F32) * ys).astype(o_ref.dtype)


def _merge(attn, ssm, gate, wa, ws, tm=1024, tn=512):
    t = attn.shape[0]
    n = wa.shape[1]
    nj = n // tn
    return pl.pallas_call(
        _merge_kernel,
        grid=(t // tm, nj),
        in_specs=[pl.BlockSpec((tm, attn.shape[1]), lambda i, j: (i, 0)),
                  pl.BlockSpec((tm, ssm.shape[1]), lambda i, j: (i, 0)),
                  pl.BlockSpec((tm, tn), lambda i, j: (i, j)),
                  pl.BlockSpec((tm, tn), lambda i, j: (i, j + nj)),
                  pl.BlockSpec((wa.shape[0], tn), lambda i, j: (0, j)),
                  pl.BlockSpec((ws.shape[0], tn), lambda i, j: (0, j))],
        out_specs=pl.BlockSpec((tm, tn), lambda i, j: (i, j)),
        out_shape=jax.ShapeDtypeStruct((t, n), BF16),
        compiler_params=_params("parallel", "parallel"),
        name="merge",
    )(attn, ssm, gate, gate, wa, ws)


def _out_mlp_kernel(x_ref, m_ref, wo_ref, nw_ref, wu_ref, wd_ref, o_ref, hn_ref):
    j = pl.program_id(1)

    @pl.when(j == 0)
    def _():
        h = x_ref[...] + _dot(m_ref[...], wo_ref[...])
        ms = jnp.mean(h * h, axis=-1, keepdims=True)
        hn_ref[...] = (h * lax.rsqrt(ms + EPS) * nw_ref[...]).astype(hn_ref.dtype)
        o_ref[...] = h

    u = _dot(hn_ref[...], wu_ref[...])
    act = jnp.square(jnp.maximum(u, 0.0)).astype(BF16)
    o_ref[...] += _dot(act, wd_ref[...])


def _out_mlp(x, merged, w_out, norm_w, w_up, w_down, tm=512, tf=1024):
    t, d = x.shape
    f = w_up.shape[1]
    return pl.pallas_call(
        _out_mlp_kernel,
        grid=(t // tm, f // tf),
        in_specs=[pl.BlockSpec((tm, d), lambda i, j: (i, 0)),
                  pl.BlockSpec((tm, merged.shape[1]), lambda i, j: (i, 0)),
                  pl.BlockSpec(w_out.shape, lambda i, j: (0, 0), pipeline_mode=pl.Buffered(1)),
                  pl.BlockSpec((1, d), lambda i, j: (0, 0)),
                  pl.BlockSpec((d, tf), lambda i, j: (0, j)),
                  pl.BlockSpec((tf, d), lambda i, j: (j, 0))],
        out_specs=pl.BlockSpec((tm, d), lambda i, j: (i, 0)),
        out_shape=jax.ShapeDtypeStruct((t, d), F32),
        scratch_shapes=[pltpu.VMEM((tm, d), BF16)],
        compiler_params=_params("parallel", "arbitrary"),
        name="out_mlp",
    )(x, merged, w_out, norm_w.reshape(1, d), w_up, w_down)


def _layer(h, bsz, seq, mix_norm_w, w_in, q_norm_w, k_norm_w, conv_w, conv_b, dt_bias, a_log, d_skip,
           ssm_norm_w, w_attn_out, w_ssm_out, w_out, mlp_norm_w, w_up, w_down, slopes):
    o_k = ATTN_WIDTH
    o_v = 2 * ATTN_WIDTH
    o_z = 3 * ATTN_WIDTH
    o_x = o_z + SSM_INNER
    o_dt = o_x + SSM_CONV_DIM
    o_g = o_dt + SSM_HEADS
    assert o_g - o_dt == GATE_SKEW and o_dt % PROJ_TN == 0
    w_t = jnp.swapaxes(w_in, 0, 1)
    wdt_t = w_t[o_dt:o_g].astype(BF16)
    head_col = pl.BlockSpec((ATTN_HEAD_DIM, 1), lambda j, i: (0, 0))
    head_row = pl.BlockSpec((1, ATTN_HEAD_DIM), lambda j, i: (0, 0))
    tn = PROJ_TN
    skew_blocks = tn // GATE_SKEW

    hn, dt, dtt = _rmsnorm_dt(h, mix_norm_w, wdt_t.T, wdt_t)
    qt = _project(hn, w_t, _proj_qnorm_t_kernel, n=ATTN_WIDTH, transposed=True,
                  extra=(q_norm_w.reshape(ATTN_HEAD_DIM, 1),), extra_specs=(head_col,), name="proj_q")
    k, wo_bf, wa_bf, ws_bf = _project(hn, w_t, _proj_knorm_kernel, n=ATTN_WIDTH, col0=o_k,
                                      extra=(k_norm_w.reshape(1, ATTN_HEAD_DIM),), extra_specs=(head_row,),
                                      to_round=(w_out, w_attn_out, w_ssm_out), name="proj_k")
    vt = _project(hn, w_t, _proj_t_kernel, n=ATTN_WIDTH, col0=o_v, transposed=True, name="proj_v")
    zact, wu_bf = _project(hn, w_t, functools.partial(_proj_act_kernel, act=_silu), n=SSM_INNER, col0=o_z,
                           to_round=(w_up,), name="proj_z")
    act = _project(hn, w_t, functools.partial(_proj_conv_kernel, seq=seq, sub=CONV_SUB), n=SSM_CONV_DIM, col0=o_x,
                   extra=(conv_w, conv_b.reshape(1, -1)),
                   extra_specs=(pl.BlockSpec((SSM_CONV, tn), lambda j, i: (0, j)),
                                pl.BlockSpec((1, tn), lambda j, i: (0, j))),
                   scratch=(pltpu.VMEM((CONV_HALO, tn), F32),), name="proj_xbc")
    gates, wd_bf = _project(hn, w_t, _proj_gate_kernel, n=N_GATES, col0=o_dt, extra=(w_t,),
                     extra_specs=(pl.BlockSpec((GATE_SKEW, w_t.shape[1]),
                                               lambda j, i: (o_dt // GATE_SKEW + (j + 1) * skew_blocks, 0)),),
                     to_round=(w_down,), name="proj_gate")

    attn = _moba_attention(qt, k, vt, slopes, bsz, seq)[0]
    ssm = _ssd_branch(act, zact, dt, dtt, dt_bias, a_log, d_skip, ssm_norm_w, bsz, seq)
    merged = _merge(attn, ssm, gates, wa_bf, ws_bf)
    return _out_mlp(h, merged, wo_bf, mlp_norm_w, wu_bf, wd_bf)


def kernel(x, mix_norm_w, w_in, q_norm_w, k_norm_w, conv_w, conv_b, dt_bias, a_log, d_skip, ssm_norm_w,
           w_attn_out, w_ssm_out, w_out, mlp_norm_w, w_up, w_down):
    bsz, seq, d = x.shape
    slopes = _slope_table()
    h = x.reshape(bsz * seq, d)
    for layer in range(w_in.shape[0]):
        h = _layer(h, bsz, seq, mix_norm_w[layer], w_in[layer], q_norm_w[layer], k_norm_w[layer], conv_w[layer],
                   conv_b[layer], dt_bias[layer], a_log[layer], d_skip[layer], ssm_norm_w[layer], w_attn_out[layer],
                   w_ssm_out[layer], w_out[layer], mlp_norm_w[layer], w_up[layer], w_down[layer], slopes)
    return h.reshape(bsz, seq, d)
```

```python
import functools

import jax
import jax.numpy as jnp
from jax import lax
from jax.experimental import pallas as pl
from jax.experimental.pallas import tpu as pltpu

D_MODEL = 2048
ATTN_HEAD_DIM = 128
ATTN_HEADS = D_MODEL // ATTN_HEAD_DIM
ATTN_WIDTH = ATTN_HEADS * ATTN_HEAD_DIM
MOBA_BLOCK = 256
MOBA_TOPK = 3
SSM_INNER = 2 * D_MODEL
SSM_HEAD_DIM = 64
SSM_HEADS = SSM_INNER // SSM_HEAD_DIM
SSM_GROUPS = 8
SSM_STATE = 128
SSM_CONV = 4
SSM_CHUNK = 128
SSM_BC = SSM_GROUPS * SSM_STATE
SSM_CONV_DIM = SSM_INNER + 2 * SSM_BC
GROUP_WIDTH = SSM_INNER // SSM_GROUPS
D_FF = 4 * D_MODEL
EPS = 1e-6
NEG_INF = -1e30
LOG2E = 1.4426950408889634
Q_SCALE = ATTN_HEAD_DIM ** -0.5 * LOG2E
SLOPE_TERMS = 3
AUG_COLS = 16
SUM_ROWS = 16
BF16_SUBLANES = 16
SSD_CHUNKS_PER_STEP = 4

LANES = 128
CONV_HALO = 8
PROJ_TM = 1024
PROJ_TN = 1024
CONV_SUB = 256
N_GATES = 2 * D_MODEL
GATE_SKEW = SSM_HEADS
VMEM_LIMIT = 56 * 1024 * 1024

F32 = jnp.float32
BF16 = jnp.bfloat16

_NT = (((1,), (1,)), ((), ()))


def _params(*sem):
    return pltpu.CompilerParams(dimension_semantics=sem, vmem_limit_bytes=VMEM_LIMIT)


def _dot(a, b):
    return jnp.dot(a, b, preferred_element_type=F32)


def _dot_nt(a, b):
    return lax.dot_general(a, b, _NT, preferred_element_type=F32)


def _rmsnorm_dt_kernel(x_ref, w_ref, wdt_ref, wdtt_ref, o_ref, dt_ref, dtt_ref):
    x = x_ref[...]
    ms = jnp.mean(x * x, axis=-1, keepdims=True)
    hn = (x * lax.rsqrt(ms + EPS) * w_ref[...]).astype(o_ref.dtype)
    o_ref[...] = hn
    dt_ref[...] = _dot(hn, wdt_ref[...])
    dtt_ref[...] = _dot_nt(wdtt_ref[...], hn)


def _rmsnorm_dt(x, w, wdt, wdt_t, tm=512):
    t, d = x.shape
    n = wdt.shape[1]
    return pl.pallas_call(
        _rmsnorm_dt_kernel,
        grid=(t // tm,),
        in_specs=[pl.BlockSpec((tm, d), lambda i: (i, 0)), pl.BlockSpec((1, d), lambda i: (0, 0)),
                  pl.BlockSpec((d, n), lambda i: (0, 0)), pl.BlockSpec((n, d), lambda i: (0, 0))],
        out_specs=[pl.BlockSpec((tm, d), lambda i: (i, 0)), pl.BlockSpec((tm, n), lambda i: (i, 0)),
                   pl.BlockSpec((n, tm), lambda i: (0, i))],
        out_shape=[jax.ShapeDtypeStruct((t, d), BF16), jax.ShapeDtypeStruct((t, n), F32),
                   jax.ShapeDtypeStruct((n, t), F32)],
        compiler_params=_params("parallel"),
        name="rmsnorm_dt",
    )(x, w.reshape(1, d), wdt, wdt_t)


def _stage_weight(w_ref, wbf_ref, transpose=True):
    @pl.when(pl.program_id(1) == 0)
    def _():
        w = w_ref[...]
        wbf_ref[...] = (w.T if transpose else w).astype(BF16)


def _silu(x):
    half = 0.5 * x
    return half + half * jnp.tanh(half)


def _proj_act_kernel(a_ref, w_ref, o_ref, wbf_ref, *, act):
    _stage_weight(w_ref, wbf_ref)
    o_ref[...] = act(_dot(a_ref[...], wbf_ref[...])).astype(o_ref.dtype)


def _proj_gate_kernel(a_ref, w_ref, wnext_ref, o_ref, wbf_ref):
    @pl.when(pl.program_id(1) == 0)
    def _():
        w = jnp.concatenate([w_ref[GATE_SKEW:, :], wnext_ref[...]], axis=0)
        wbf_ref[...] = w.T.astype(BF16)

    o_ref[...] = (0.5 + 0.5 * jnp.tanh(0.5 * _dot(a_ref[...], wbf_ref[...]))).astype(o_ref.dtype)


def _proj_t_kernel(a_ref, wt_ref, o_ref, wbf_ref):
    _stage_weight(wt_ref, wbf_ref, transpose=False)
    o_ref[...] = _dot_nt(wbf_ref[...], a_ref[...]).astype(o_ref.dtype)


def _proj_knorm_kernel(a_ref, w_ref, nw_ref, o_ref, wbf_ref):
    _stage_weight(w_ref, wbf_ref)
    acc = _dot(a_ref[...], wbf_ref[...])
    nw = nw_ref[...]
    for h in range(acc.shape[1] // ATTN_HEAD_DIM):
        slab = acc[:, h * ATTN_HEAD_DIM:(h + 1) * ATTN_HEAD_DIM]
        ms = jnp.mean(slab * slab, axis=-1, keepdims=True)
        o_ref[:, h * ATTN_HEAD_DIM:(h + 1) * ATTN_HEAD_DIM] = (slab * lax.rsqrt(ms + EPS) * nw).astype(o_ref.dtype)


def _proj_qnorm_t_kernel(a_ref, wt_ref, nw_ref, o_ref, wbf_ref):
    _stage_weight(wt_ref, wbf_ref, transpose=False)
    acc = _dot_nt(wbf_ref[...], a_ref[...])
    nw = nw_ref[...]
    for h in range(acc.shape[0] // ATTN_HEAD_DIM):
        slab = acc[h * ATTN_HEAD_DIM:(h + 1) * ATTN_HEAD_DIM, :]
        ms = jnp.mean(slab * slab, axis=0, keepdims=True)
        qn = slab * lax.rsqrt(ms + EPS) * nw
        o_ref[h * ATTN_HEAD_DIM:(h + 1) * ATTN_HEAD_DIM, :] = (qn * Q_SCALE).astype(o_ref.dtype)


def _proj_conv_kernel(a_ref, w_ref, cw_ref, cb_ref, o_ref, wbf_ref, tail_ref, *, seq, sub):
    _stage_weight(w_ref, wbf_ref)
    i = pl.program_id(1)
    tm = a_ref.shape[0]
    w = wbf_ref[...]

    @pl.when((i * tm) % seq == 0)
    def _():
        tail_ref[...] = jnp.zeros_like(tail_ref)

    tail = tail_ref[...]
    for r in range(0, tm, sub):
        cur = _dot(a_ref[r:r + sub, :], w)
        ext = jnp.concatenate([tail, cur], axis=0)
        acc = cb_ref[...] + cw_ref[SSM_CONV - 1:SSM_CONV, :] * cur
        for back in range(1, SSM_CONV):
            shifted = pltpu.roll(ext, back, axis=0)[CONV_HALO:CONV_HALO + sub, :]
            acc = acc + cw_ref[SSM_CONV - 1 - back:SSM_CONV - back, :] * shifted
        o_ref[r:r + sub, :] = _silu(acc).astype(o_ref.dtype)
        tail = cur[sub - CONV_HALO:sub, :]
    tail_ref[...] = tail


def _rounding_too(body, n_in, n_round):
    def kernel(*refs):
        slabs_in = refs[n_in:n_in + n_round]
        o_ref = refs[n_in + n_round]
        slabs_out = refs[n_in + n_round + 1:n_in + 2 * n_round + 1]
        for src_ref, dst_ref in zip(slabs_in, slabs_out):
            dst_ref[...] = src_ref[...].astype(dst_ref.dtype)
        body(*refs[:n_in], o_ref, *refs[n_in + 2 * n_round + 1:])
    return kernel


def _project(a, w, body, *, n, col0=0, transposed=False, extra=(), extra_specs=(), scratch=(), to_round=(),
             out_dtype=BF16, tm=PROJ_TM, tn=PROJ_TN, name="proj"):
    t, k = a.shape
    assert n % tn == 0 and col0 % tn == 0 and t % tm == 0
    c0 = col0 // tn
    n_i = t // tm
    n_steps = (n // tn) * n_i
    in_specs = [pl.BlockSpec((tm, k), lambda j, i: (i, 0)), pl.BlockSpec((tn, k), lambda j, i: (c0 + j, 0))]
    args = [a, w]
    if transposed:
        o_spec = pl.BlockSpec((tn, tm), lambda j, i: (j, i))
        o_shape = (n, t)
        w_block = (tn, k)
    else:
        o_spec = pl.BlockSpec((tm, tn), lambda j, i: (i, j))
        o_shape = (t, n)
        w_block = (k, tn)
    slab_specs = []
    for r in to_round:
        rows = r.shape[0] // n_steps
        assert rows * n_steps == r.shape[0] and rows % BF16_SUBLANES == 0
        slab_specs.append(pl.BlockSpec((rows, r.shape[1]), lambda j, i: (j * n_i + i, 0)))
    out = pl.pallas_call(
        _rounding_too(body, 2 + len(extra), len(to_round)),
        grid=(n // tn, n_i),
        in_specs=in_specs + list(extra_specs) + slab_specs,
        out_specs=[o_spec] + slab_specs,
        out_shape=[jax.ShapeDtypeStruct(o_shape, out_dtype)] + [jax.ShapeDtypeStruct(r.shape, BF16) for r in to_round],
        scratch_shapes=[pltpu.VMEM(w_block, BF16)] + list(scratch),
        compiler_params=_params("parallel", "arbitrary"),
        name=name,
    )(*args, *extra, *to_round)
    return out if to_round else out[0]


def _moba_kernel(*refs, n_cast):
    slopes_ref, qt_ref, k_ref, vt_ref = refs[:4]
    cast_in = refs[4:4 + n_cast]
    o_ref = refs[4 + n_cast]
    cast_out = refs[5 + n_cast:5 + 2 * n_cast]
    (kaug_ref, qaug_ref, kmh_ref, kml_ref, sel_ref, acc_ref, sd_ref, s0_ref, s1_ref,
     vaug_ref) = refs[5 + 2 * n_cast:]
    for src_ref, dst_ref in zip(cast_in, cast_out):
        dst_ref[...] = src_ref[...].astype(dst_ref.dtype)

    hg = pl.program_id(1)
    i = pl.program_id(2)
    blk = MOBA_BLOCK
    dh = ATTN_HEAD_DIM
    seq = k_ref.shape[0]
    nb = seq // blk
    n_chain = qt_ref.shape[0] // dh
    n_split = SLOPE_TERMS

    @pl.when(i == 0)
    def _():
        r = lax.broadcasted_iota(jnp.int32, (nb, seq), 0)
        c = lax.broadcasted_iota(jnp.int32, (nb, seq), 1)
        lo = r * blk
        pool = jnp.where((c >= lo) & (c < lo + blk), 1.0 / blk, 0.0).astype(BF16)
        k_lane = lax.broadcasted_iota(jnp.int32, (blk, AUG_COLS), 1)
        k_row = lax.broadcasted_iota(jnp.int32, (blk, AUG_COLS), 0).astype(F32)
        q_row = lax.broadcasted_iota(jnp.int32, (AUG_COLS, blk), 0)
        q_lane = lax.broadcasted_iota(jnp.int32, (AUG_COLS, blk), 1).astype(F32)
        ones_row = lax.broadcasted_iota(jnp.int32, (SUM_ROWS, seq), 0) == 0
        for g in range(n_chain):
            vaug_ref[g, 0:dh, :] = vt_ref[g * dh:(g + 1) * dh, :]
            vaug_ref[g, dh:dh + SUM_ROWS, :] = jnp.where(ones_row, 1.0, 0.0).astype(BF16)
            kg = k_ref[:, g * dh:(g + 1) * dh]
            kmean = _dot(pool, kg)
            hi = kmean.astype(BF16)
            kmh_ref[g] = hi
            kml_ref[g] = (kmean - hi.astype(F32)).astype(BF16)
            kaug_ref[g, :, 0:dh] = kg
            k_aug = jnp.where((k_lane >= n_split) & (k_lane < 2 * n_split), k_row, 0.0)
            q_aug = jnp.where(q_row < n_split, -q_lane, 0.0)
            for t in range(n_split):
                s_t = slopes_ref[(hg * n_chain + g) * n_split + t]
                k_aug = jnp.where(k_lane == t, s_t, k_aug)
                q_aug = jnp.where(q_row == n_split + t, s_t, q_aug)
            k_aug = k_aug.astype(BF16)
            for jb in range(nb):
                kaug_ref[g, jb * blk:(jb + 1) * blk, dh:dh + AUG_COLS] = k_aug
            qaug_ref[g, dh:dh + AUG_COLS, :] = q_aug.astype(BF16)

    chains = range(n_chain)

    def score_stage(j, buf_ref):
        jc = jnp.minimum(j, i)
        for g in chains:
            kb = kaug_ref[g, pl.ds(pl.multiple_of(jc * blk, blk), blk), :]
            buf_ref[g] = _dot(kb, qaug_ref[g])

    def values(g, j):
        jc = jnp.minimum(j, i)
        return vaug_ref[g, :, pl.ds(pl.multiple_of(jc * blk, blk), blk)]

    def softmax_stage(j, buf_ref, ms):
        far = ((i - j) * blk).astype(F32)
        ms_new = []
        for g in chains:
            m = ms[g]
            c_shift = slopes_ref[(ATTN_HEADS + hg * n_chain + g) * n_split] * far
            keep = sel_ref[g, pl.ds(jnp.minimum(j, nb - 1), 1), :] > 0.0
            m_new = jnp.maximum(m, jnp.where(keep, jnp.max(buf_ref[g], axis=0, keepdims=True) - c_shift, NEG_INF))
            alpha = jnp.exp2(m - m_new)
            p = jnp.exp2(buf_ref[g] - jnp.where(keep, m_new + c_shift, -NEG_INF))
            ms_new.append(m_new)
            acc_ref[g] = alpha * acc_ref[g] + _dot(values(g, j), p.astype(BF16))
        return tuple(ms_new)

    for g in chains:
        qaug_ref[g, 0:dh, :] = qt_ref[g * dh:(g + 1) * dh, :]
    score_stage(i, sd_ref)
    score_stage(0, s0_ref)

    jidx = lax.broadcasted_iota(jnp.int32, (nb, blk), 0)
    for g in chains:
        qt = qt_ref[g * dh:(g + 1) * dh, :]
        gate = _dot(kmh_ref[g], qt) + _dot(kml_ref[g], qt)
        cnt = jnp.zeros((nb, blk), F32)
        for jp in range(nb):
            row = gate[jp:jp + 1, :]
            beats = (row > gate) | ((row == gate) & (jidx > jp))
            cnt = cnt + jnp.where(beats, (jp < i).astype(F32), 0.0)
        sel_ref[g] = jnp.where((cnt < MOBA_TOPK) & (jidx < i), 1.0, 0.0)

    kl = lax.broadcasted_iota(jnp.int32, (blk, blk), 0)
    ql = lax.broadcasted_iota(jnp.int32, (blk, blk), 1)
    ms = []
    for g in chains:
        s = jnp.where(kl <= ql, sd_ref[g], NEG_INF)
        m0 = jnp.max(s, axis=0, keepdims=True)
        p = jnp.exp2(s - m0)
        ms.append(m0)
        acc_ref[g] = _dot(values(g, i), p.astype(BF16))

    def body(t, ms):
        j = 2 * t
        score_stage(j + 1, s1_ref)
        ms = softmax_stage(j, s0_ref, ms)
        score_stage(j + 2, s0_ref)
        return softmax_stage(j + 1, s1_ref, ms)

    ms = lax.fori_loop(0, i // 2, body, tuple(ms))

    @pl.when(i % 2 == 1)
    def _():
        softmax_stage(i - 1, s0_ref, ms)

    for g in chains:
        acc = acc_ref[g]
        o_ref[:, g * dh:(g + 1) * dh] = (acc[0:dh] / acc[dh:dh + 1]).T.astype(o_ref.dtype)


def _moba_attention(qt, k, vt, slope_table, bsz, seq, to_round=(), heads_per_step=4):
    assert seq % MOBA_BLOCK == 0 and ATTN_HEADS % heads_per_step == 0
    nq = seq // MOBA_BLOCK
    nb = nq
    dh = ATTN_HEAD_DIM
    g = heads_per_step
    n_hg = ATTN_HEADS // g
    n_steps = bsz * n_hg * nq
    slab_specs = []
    for w in to_round:
        rows = w.shape[0] // n_steps
        assert rows * n_steps == w.shape[0] and rows % BF16_SUBLANES == 0
        slab_specs.append(pl.BlockSpec((rows, w.shape[1]), lambda b, h, i: ((b * n_hg + h) * nq + i, 0)))
    return pl.pallas_call(
        functools.partial(_moba_kernel, n_cast=len(to_round)),
        grid=(bsz, n_hg, nq),
        in_specs=[pl.BlockSpec(memory_space=pltpu.SMEM),
                  pl.BlockSpec((g * dh, MOBA_BLOCK), lambda b, h, i: (h, b * nq + i)),
                  pl.BlockSpec((seq, g * dh), lambda b, h, i: (b, h)),
                  pl.BlockSpec((g * dh, seq), lambda b, h, i: (h, b))] + slab_specs,
        out_specs=[pl.BlockSpec((MOBA_BLOCK, g * dh), lambda b, h, i: (b * nq + i, h))] + slab_specs,
        out_shape=[jax.ShapeDtypeStruct((bsz * seq, ATTN_WIDTH), BF16)]
                  + [jax.ShapeDtypeStruct(w.shape, BF16) for w in to_round],
        scratch_shapes=[pltpu.VMEM((g, seq, dh + AUG_COLS), BF16),
                        pltpu.VMEM((g, dh + AUG_COLS, MOBA_BLOCK), BF16),
                        pltpu.VMEM((g, nb, dh), BF16), pltpu.VMEM((g, nb, dh), BF16),
                        pltpu.VMEM((g, nb, MOBA_BLOCK), F32),
                        pltpu.VMEM((g, dh + SUM_ROWS, MOBA_BLOCK), F32)]
                       + [pltpu.VMEM((g, MOBA_BLOCK, MOBA_BLOCK), F32)] * 3
                       + [pltpu.VMEM((g, dh + SUM_ROWS, seq), BF16)],
        compiler_params=_params("parallel", "parallel", "arbitrary"),
        name="moba_attention",
    )(slope_table, qt, k, vt, *to_round)


def _slope_table():
    slopes = jnp.exp2(-8.0 * jnp.arange(1, ATTN_HEADS + 1, dtype=F32) / ATTN_HEADS) * LOG2E
    terms, rest = [], slopes
    for _ in range(SLOPE_TERMS):
        t = rest.astype(BF16).astype(F32)
        terms.append(t)
        rest = rest - t
    split = jnp.stack(terms, axis=1).reshape(-1)
    full = jnp.stack([slopes] * SLOPE_TERMS, axis=1).reshape(-1)
    return jnp.concatenate([split, full])


def _split3(x):
    h1 = x.astype(BF16)
    r1 = x - h1.astype(F32)
    h2 = r1.astype(BF16)
    h3 = (r1 - h2.astype(F32)).astype(BF16)
    return h1, h2, h3


def _softplus(x):
    return jnp.maximum(x, 0.0) + jnp.log1p(jnp.exp(-jnp.abs(x)))


def _ssd_kernel(act_ref, z_ref, dt_ref, dtt_ref, dtb_ref, dtbt_ref, alog_ref, alogt_ref,
                dskip_ref, normw_ref, o_ref, state_ref, y_ref):
    c = pl.program_id(1)
    ln = SSM_CHUNK

    @pl.when(c == 0)
    def _():
        state_ref[...] = jnp.zeros_like(state_ref)

    row = lax.broadcasted_iota(jnp.int32, (ln, ln), 0)
    col = lax.broadcasted_iota(jnp.int32, (ln, ln), 1)
    causal = row >= col
    tri = jnp.where(causal, 1.0, 0.0).astype(BF16)
    trit = jnp.where(row <= col, 1.0, 0.0).astype(BF16)
    left = lax.broadcasted_iota(jnp.int32, (1, LANES), 1) < SSM_HEAD_DIM
    left_full = lax.broadcasted_iota(jnp.int32, (ln, LANES), 1) < SSM_HEAD_DIM
    pairs_per_group = GROUP_WIDTH // LANES

    for cc in range(act_ref.shape[0] // ln):
        rows = slice(cc * ln, (cc + 1) * ln)
        dt = _softplus(dt_ref[rows, :] + dtb_ref[...])
        dtt = _softplus(dtt_ref[:, rows] + dtbt_ref[...])
        la = dt * (-jnp.exp(alog_ref[...]))
        lat = dtt * (-jnp.exp(alogt_ref[...]))
        a1, a2, a3 = _split3(la)
        acum = (_dot(tri, a1) + _dot(tri, a2) + _dot(tri, a3)) * LOG2E
        b1, b2, b3 = _split3(lat)
        acumt = (_dot(b1, trit) + _dot(b2, trit) + _dot(b3, trit)) * LOG2E
        rowdt = acumt - jnp.log2(dtt)

        for g in range(SSM_GROUPS):
            b0 = SSM_INNER + g * SSM_STATE
            c0 = SSM_INNER + SSM_BC + g * SSM_STATE
            bgb = act_ref[rows, b0:b0 + SSM_STATE]
            cgb = act_ref[rows, c0:c0 + SSM_STATE]
            cb = _dot_nt(cgb, bgb)
            bgt = bgb.astype(F32).T
            for pp in range(pairs_per_group):
                p = g * pairs_per_group + pp
                x2 = act_ref[rows, p * LANES:(p + 1) * LANES]
                x_heads = (jnp.where(left_full, x2, 0), jnp.where(left_full, 0, x2))
                ops = []
                eacs = []
                for hh in (2 * p, 2 * p + 1):
                    lhs = []
                    colb = jnp.broadcast_to(acum[:, hh:hh + 1], (ln, ln))
                    rowb = rowdt[hh:hh + 1, :]
                    decay_dt = jnp.exp2(jnp.where(causal, colb - rowb, -jnp.inf))
                    lhs.append((cb * decay_dt).astype(BF16))
                    wt = jnp.exp2(acumt[hh:hh + 1, ln - 1:ln] - rowb)
                    lhs.append((bgt * wt).astype(BF16))
                    eacs.append(jnp.exp2(colb))
                    ops.append(jnp.concatenate(lhs, axis=0))
                res = _dot(ops[0], x_heads[0]) + _dot(ops[1], x_heads[1])
                eac = jnp.where(left, eacs[0], eacs[1])
                st = state_ref[p]
                y = res[0:ln] + _dot(cgb, st.astype(BF16)) * eac
                y_ref[:, p * LANES:(p + 1) * LANES] = y
                state_ref[p] = st * eac[ln - 1:ln, :] + res[ln:2 * ln]
            g0 = g * GROUP_WIDTH
            xg = act_ref[rows, g0:g0 + GROUP_WIDTH].astype(F32)
            yg = y_ref[:, g0:g0 + GROUP_WIDTH] + xg * dskip_ref[:, g0:g0 + GROUP_WIDTH]
            yg = yg * z_ref[rows, g0:g0 + GROUP_WIDTH].astype(F32)
            ms = jnp.mean(yg * yg, axis=-1, keepdims=True)
            o_ref[rows, g0:g0 + GROUP_WIDTH] = (yg * lax.rsqrt(ms + EPS)
                                                * normw_ref[:, g0:g0 + GROUP_WIDTH]).astype(o_ref.dtype)


def _ssd_branch(act, zact, dt, dtt, dt_bias, a_log, d_skip, norm_w, bsz, seq):
    ln = SSM_CHUNK * SSD_CHUNKS_PER_STEP
    assert seq % ln == 0
    nc = seq // ln
    hs = SSM_HEADS
    row = lambda b, c: (b * nc + c, 0)
    fixed = lambda b, c: (0, 0)
    return pl.pallas_call(
        _ssd_kernel,
        grid=(bsz, nc),
        in_specs=[pl.BlockSpec((ln, SSM_CONV_DIM), row),
                  pl.BlockSpec((ln, SSM_INNER), row),
                  pl.BlockSpec((ln, hs), row),
                  pl.BlockSpec((hs, ln), lambda b, c: (0, b * nc + c)),
                  pl.BlockSpec((1, hs), fixed),
                  pl.BlockSpec((hs, 1), fixed),
                  pl.BlockSpec((1, hs), fixed),
                  pl.BlockSpec((hs, 1), fixed),
                  pl.BlockSpec((1, SSM_INNER), fixed),
                  pl.BlockSpec((1, SSM_INNER), fixed)],
        out_specs=pl.BlockSpec((ln, SSM_INNER), row),
        out_shape=jax.ShapeDtypeStruct((bsz * seq, SSM_INNER), BF16),
        scratch_shapes=[pltpu.VMEM((SSM_HEADS // 2, SSM_STATE, LANES), F32),
                        pltpu.VMEM((SSM_CHUNK, SSM_INNER), F32)],
        compiler_params=_params("arbitrary", "arbitrary"),
        name="ssd",
    )(act, zact, dt, dtt, dt_bias.reshape(1, hs), dt_bias.reshape(hs, 1),
      a_log.reshape(1, hs), a_log.reshape(hs, 1),
      jnp.repeat(d_skip, SSM_HEAD_DIM).reshape(1, SSM_INNER), norm_w.reshape(1, SSM_INNER))


def _merge_kernel(attn_ref, ssm_ref, ga_ref, gs_ref, wa_ref, ws_ref, o_ref):
    ya = _dot(attn_ref[...], wa_ref[...])
    ys = _dot(ssm_ref[...], ws_ref[...])
    o_ref[...] = (ga_ref[...].astype(F32) * ya + gs_ref[...].astype(F32) * ys).astype(o_ref.dtype)


def _merge(attn, ssm, gate, wa, ws, tm=1024, tn=512):
    t = attn.shape[0]
    n = wa.shape[1]
    nj = n // tn
    return pl.pallas_call(
        _merge_kernel,
        grid=(t // tm, nj),
        in_specs=[pl.BlockSpec((tm, attn.shape[1]), lambda i, j: (i, 0)),
                  pl.BlockSpec((tm, ssm.shape[1]), lambda i, j: (i, 0)),
                  pl.BlockSpec((tm, tn), lambda i, j: (i, j)),
                  pl.BlockSpec((tm, tn), lambda i, j: (i, j + nj)),
                  pl.BlockSpec((wa.shape[0], tn), lambda i, j: (0, j)),
                  pl.BlockSpec((ws.shape[0], tn), lambda i, j: (0, j))],
        out_specs=pl.BlockSpec((tm, tn), lambda i, j: (i, j)),
        out_shape=jax.ShapeDtypeStruct((t, n), BF16),
        compiler_params=_params("parallel", "parallel"),
        name="merge",
    )(attn, ssm, gate, gate, wa, ws)


def _out_mlp_kernel(x_ref, m_ref, wo_ref, nw_ref, wu_ref, wd_ref, o_ref, hn_ref):
    j = pl.program_id(1)

    @pl.when(j == 0)
    def _():
        h = x_ref[...] + _dot(m_ref[...], wo_ref[...])
        ms = jnp.mean(h * h, axis=-1, keepdims=True)
        hn_ref[...] = (h * lax.rsqrt(ms + EPS) * nw_ref[...]).astype(hn_ref.dtype)
        o_ref[...] = h

    u = _dot(hn_ref[...], wu_ref[...])
    act = jnp.square(jnp.maximum(u, 0.0)).astype(BF16)
    o_ref[...] += _dot(act, wd_ref[...])


def _out_mlp(x, merged, w_out, norm_w, w_up, w_down, tm=512, tf=1024):
    t, d = x.shape
    f = w_up.shape[1]
    return pl.pallas_call(
        _out_mlp_kernel,
        grid=(t // tm, f // tf),
        in_specs=[pl.BlockSpec((tm, d), lambda i, j: (i, 0)),
                  pl.BlockSpec((tm, merged.shape[1]), lambda i, j: (i, 0)),
                  pl.BlockSpec(w_out.shape, lambda i, j: (0, 0), pipeline_mode=pl.Buffered(1)),
                  pl.BlockSpec((1, d), lambda i, j: (0, 0)),
                  pl.BlockSpec((d, tf), lambda i, j: (0, j)),
                  pl.BlockSpec((tf, d), lambda i, j: (j, 0))],
        out_specs=pl.BlockSpec((tm, d), lambda i, j: (i, 0)),
        out_shape=jax.ShapeDtypeStruct((t, d), F32),
        scratch_shapes=[pltpu.VMEM((tm, d), BF16)],
        compiler_params=_params("parallel", "arbitrary"),
        name="out_mlp",
    )(x, merged, w_out, norm_w.reshape(1, d), w_up, w_down)


def _layer(h, bsz, seq, mix_norm_w, w_in, q_norm_w, k_norm_w, conv_w, conv_b, dt_bias, a_log, d_skip,
           ssm_norm_w, w_attn_out, w_ssm_out, w_out, mlp_norm_w, w_up, w_down, slopes):
    o_k = ATTN_WIDTH
    o_v = 2 * ATTN_WIDTH
    o_z = 3 * ATTN_WIDTH
    o_x = o_z + SSM_INNER
    o_dt = o_x + SSM_CONV_DIM
    o_g = o_dt + SSM_HEADS
    assert o_g - o_dt == GATE_SKEW and o_dt % PROJ_TN == 0
    w_t = jnp.swapaxes(w_in, 0, 1)
    wdt_t = w_t[o_dt:o_g].astype(BF16)
    head_col = pl.BlockSpec((ATTN_HEAD_DIM, 1), lambda j, i: (0, 0))
    head_row = pl.BlockSpec((1, ATTN_HEAD_DIM), lambda j, i: (0, 0))
    tn = PROJ_TN
    skew_blocks = tn // GATE_SKEW

    hn, dt, dtt = _rmsnorm_dt(h, mix_norm_w, wdt_t.T, wdt_t)
    qt = _project(hn, w_t, _proj_qnorm_t_kernel, n=ATTN_WIDTH, transposed=True,
                  extra=(q_norm_w.reshape(ATTN_HEAD_DIM, 1),), extra_specs=(head_col,), name="proj_q")
    k = _project(hn, w_t, _proj_knorm_kernel, n=ATTN_WIDTH, col0=o_k,
                 extra=(k_norm_w.reshape(1, ATTN_HEAD_DIM),), extra_specs=(head_row,), name="proj_k")
    vt, wo_bf, wa_bf, ws_bf = _project(hn, w_t, _proj_t_kernel, n=ATTN_WIDTH, col0=o_v, transposed=True,
                                       to_round=(w_out, w_attn_out, w_ssm_out), name="proj_v")
    zact, wu_bf = _project(hn, w_t, functools.partial(_proj_act_kernel, act=_silu), n=SSM_INNER, col0=o_z,
                           to_round=(w_up,), name="proj_z")
    act = _project(hn, w_t, functools.partial(_proj_conv_kernel, seq=seq, sub=CONV_SUB), n=SSM_CONV_DIM, col0=o_x,
                   extra=(conv_w, conv_b.reshape(1, -1)),
                   extra_specs=(pl.BlockSpec((SSM_CONV, tn), lambda j, i: (0, j)),
                                pl.BlockSpec((1, tn), lambda j, i: (0, j))),
                   scratch=(pltpu.VMEM((CONV_HALO, tn), F32),), name="proj_xbc")
    gates, wd_bf = _project(hn, w_t, _proj_gate_kernel, n=N_GATES, col0=o_dt, extra=(w_t,),
                     extra_specs=(pl.BlockSpec((GATE_SKEW, w_t.shape[1]),
                                               lambda j, i: (o_dt // GATE_SKEW + (j + 1) * skew_blocks, 0)),),
                     to_round=(w_down,), name="proj_gate")

    attn = _moba_attention(qt, k, vt, slopes, bsz, seq)[0]
    ssm = _ssd_branch(act, zact, dt, dtt, dt_bias, a_log, d_skip, ssm_norm_w, bsz, seq)
    merged = _merge(attn, ssm, gates, wa_bf, ws_bf)
    return _out_mlp(h, merged, wo_bf, mlp_norm_w, wu_bf, wd_bf)


def kernel(x, mix_norm_w, w_in, q_norm_w, k_norm_w, conv_w, conv_b, dt_bias, a_log, d_skip, ssm_norm_w,
           w_attn_out, w_ssm_out, w_out, mlp_norm_w, w_up, w_down):
    bsz, seq, d = x.shape
    slopes = _slope_table()
    h = x.reshape(bsz * seq, d)
    for layer in range(w_in.shape[0]):
        h = _layer(h, bsz, seq, mix_norm_w[layer], w_in[layer], q_norm_w[layer], k_norm_w[layer], conv_w[layer],
                   conv_b[layer], dt_bias[layer], a_log[layer], d_skip[layer], ssm_norm_w[layer], w_attn_out[layer],
                   w_ssm_out[layer], w_out[layer], mlp_norm_w[layer], w_up[layer], w_down[layer], slopes)
    return h.reshape(bsz, seq, d)
```

```python
import functools

import jax
import jax.numpy as jnp
from jax import lax
from jax.experimental import pallas as pl
from jax.experimental.pallas import tpu as pltpu

D_MODEL = 2048
ATTN_HEAD_DIM = 128
ATTN_HEADS = D_MODEL // ATTN_HEAD_DIM
ATTN_WIDTH = ATTN_HEADS * ATTN_HEAD_DIM
MOBA_BLOCK = 256
MOBA_TOPK = 3
SSM_INNER = 2 * D_MODEL
SSM_HEAD_DIM = 64
SSM_HEADS = SSM_INNER // SSM_HEAD_DIM
SSM_GROUPS = 8
SSM_STATE = 128
SSM_CONV = 4
SSM_CHUNK = 128
SSM_BC = SSM_GROUPS * SSM_STATE
SSM_CONV_DIM = SSM_INNER + 2 * SSM_BC
GROUP_WIDTH = SSM_INNER // SSM_GROUPS
D_FF = 4 * D_MODEL
EPS = 1e-6
NEG_INF = -1e30
LOG2E = 1.4426950408889634
Q_SCALE = ATTN_HEAD_DIM ** -0.5 * LOG2E
SLOPE_TERMS = 3
AUG_COLS = 16
SUM_ROWS = 16
BF16_SUBLANES = 16
SSD_CHUNKS_PER_STEP = 4

LANES = 128
CONV_HALO = 8
PROJ_TM = 1024
PROJ_TN = 1024
CONV_SUB = 256
CONV_STRIP = 512
N_GATES = 2 * D_MODEL
GATE_SKEW = SSM_HEADS
VMEM_LIMIT = 56 * 1024 * 1024

F32 = jnp.float32
BF16 = jnp.bfloat16

_NT = (((1,), (1,)), ((), ()))


def _params(*sem):
    return pltpu.CompilerParams(dimension_semantics=sem, vmem_limit_bytes=VMEM_LIMIT)


def _dot(a, b):
    return jnp.dot(a, b, preferred_element_type=F32)


def _dot_nt(a, b):
    return lax.dot_general(a, b, _NT, preferred_element_type=F32)


def _rmsnorm_dt_kernel(x_ref, w_ref, wdt_ref, wdtt_ref, o_ref, dt_ref, dtt_ref):
    x = x_ref[...]
    ms = jnp.mean(x * x, axis=-1, keepdims=True)
    hn = (x * lax.rsqrt(ms + EPS) * w_ref[...]).astype(o_ref.dtype)
    o_ref[...] = hn
    dt_ref[...] = _dot(hn, wdt_ref[...])
    dtt_ref[...] = _dot_nt(wdtt_ref[...], hn)


def _rmsnorm_dt(x, w, wdt, wdt_t, tm=512):
    t, d = x.shape
    n = wdt.shape[1]
    return pl.pallas_call(
        _rmsnorm_dt_kernel,
        grid=(t // tm,),
        in_specs=[pl.BlockSpec((tm, d), lambda i: (i, 0)), pl.BlockSpec((1, d), lambda i: (0, 0)),
                  pl.BlockSpec((d, n), lambda i: (0, 0)), pl.BlockSpec((n, d), lambda i: (0, 0))],
        out_specs=[pl.BlockSpec((tm, d), lambda i: (i, 0)), pl.BlockSpec((tm, n), lambda i: (i, 0)),
                   pl.BlockSpec((n, tm), lambda i: (0, i))],
        out_shape=[jax.ShapeDtypeStruct((t, d), BF16), jax.ShapeDtypeStruct((t, n), F32),
                   jax.ShapeDtypeStruct((n, t), F32)],
        compiler_params=_params("parallel"),
        name="rmsnorm_dt",
    )(x, w.reshape(1, d), wdt, wdt_t)


def _stage_weight(w_ref, wbf_ref, transpose=True):
    @pl.when(pl.program_id(1) == 0)
    def _():
        w = w_ref[...]
        wbf_ref[...] = (w.T if transpose else w).astype(BF16)


def _silu(x):
    half = 0.5 * x
    return half + half * jnp.tanh(half)


def _proj_act_kernel(a_ref, w_ref, o_ref, wbf_ref, *, act):
    _stage_weight(w_ref, wbf_ref)
    o_ref[...] = act(_dot(a_ref[...], wbf_ref[...])).astype(o_ref.dtype)


def _proj_gate_kernel(a_ref, w_ref, wnext_ref, o_ref, wbf_ref):
    @pl.when(pl.program_id(1) == 0)
    def _():
        w = jnp.concatenate([w_ref[GATE_SKEW:, :], wnext_ref[...]], axis=0)
        wbf_ref[...] = w.T.astype(BF16)

    o_ref[...] = (0.5 + 0.5 * jnp.tanh(0.5 * _dot(a_ref[...], wbf_ref[...]))).astype(o_ref.dtype)


def _proj_t_kernel(a_ref, wt_ref, o_ref, wbf_ref):
    _stage_weight(wt_ref, wbf_ref, transpose=False)
    o_ref[...] = _dot_nt(wbf_ref[...], a_ref[...]).astype(o_ref.dtype)


def _proj_knorm_kernel(a_ref, w_ref, nw_ref, o_ref, wbf_ref):
    _stage_weight(w_ref, wbf_ref)
    acc = _dot(a_ref[...], wbf_ref[...])
    nw = nw_ref[...]
    for h in range(acc.shape[1] // ATTN_HEAD_DIM):
        slab = acc[:, h * ATTN_HEAD_DIM:(h + 1) * ATTN_HEAD_DIM]
        ms = jnp.mean(slab * slab, axis=-1, keepdims=True)
        o_ref[:, h * ATTN_HEAD_DIM:(h + 1) * ATTN_HEAD_DIM] = (slab * lax.rsqrt(ms + EPS) * nw).astype(o_ref.dtype)


def _proj_qnorm_t_kernel(a_ref, wt_ref, nw_ref, o_ref, wbf_ref):
    _stage_weight(wt_ref, wbf_ref, transpose=False)
    acc = _dot_nt(wbf_ref[...], a_ref[...])
    nw = nw_ref[...]
    for h in range(acc.shape[0] // ATTN_HEAD_DIM):
        slab = acc[h * ATTN_HEAD_DIM:(h + 1) * ATTN_HEAD_DIM, :]
        ms = jnp.mean(slab * slab, axis=0, keepdims=True)
        qn = slab * lax.rsqrt(ms + EPS) * nw
        o_ref[h * ATTN_HEAD_DIM:(h + 1) * ATTN_HEAD_DIM, :] = (qn * Q_SCALE).astype(o_ref.dtype)


def _proj_conv_kernel(a_ref, w_ref, cw_ref, cb_ref, o_ref, wbf_ref, tail_ref, *, seq, sub):
    _stage_weight(w_ref, wbf_ref)
    i = pl.program_id(1)
    tm = a_ref.shape[0]
    w = wbf_ref[...]

    @pl.when((i * tm) % seq == 0)
    def _():
        tail_ref[...] = jnp.zeros_like(tail_ref)

    for c0 in range(0, w.shape[1], CONV_STRIP):
        cols = slice(c0, c0 + CONV_STRIP)
        tail = tail_ref[:, cols]
        for r in range(0, tm, sub):
            cur = _dot(a_ref[r:r + sub, :], w[:, cols])
            ext = jnp.concatenate([tail, cur], axis=0)
            acc = cb_ref[:, cols] + cw_ref[SSM_CONV - 1:SSM_CONV, cols] * cur
            for back in range(1, SSM_CONV):
                shifted = pltpu.roll(ext, back, axis=0)[CONV_HALO:CONV_HALO + sub, :]
                acc = acc + cw_ref[SSM_CONV - 1 - back:SSM_CONV - back, cols] * shifted
            o_ref[r:r + sub, cols] = _silu(acc).astype(o_ref.dtype)
            tail = cur[sub - CONV_HALO:sub, :]
        tail_ref[:, cols] = tail


def _rounding_too(body, n_in, n_round):
    def kernel(*refs):
        slabs_in = refs[n_in:n_in + n_round]
        o_ref = refs[n_in + n_round]
        slabs_out = refs[n_in + n_round + 1:n_in + 2 * n_round + 1]
        for src_ref, dst_ref in zip(slabs_in, slabs_out):
            dst_ref[...] = src_ref[...].astype(dst_ref.dtype)
        body(*refs[:n_in], o_ref, *refs[n_in + 2 * n_round + 1:])
    return kernel


def _project(a, w, body, *, n, col0=0, transposed=False, extra=(), extra_specs=(), scratch=(), to_round=(),
             out_dtype=BF16, tm=PROJ_TM, tn=PROJ_TN, name="proj"):
    t, k = a.shape
    assert n % tn == 0 and col0 % tn == 0 and t % tm == 0
    c0 = col0 // tn
    n_i = t // tm
    n_steps = (n // tn) * n_i
    in_specs = [pl.BlockSpec((tm, k), lambda j, i: (i, 0)), pl.BlockSpec((tn, k), lambda j, i: (c0 + j, 0))]
    args = [a, w]
    if transposed:
        o_spec = pl.BlockSpec((tn, tm), lambda j, i: (j, i))
        o_shape = (n, t)
        w_block = (tn, k)
    else:
        o_spec = pl.BlockSpec((tm, tn), lambda j, i: (i, j))
        o_shape = (t, n)
        w_block = (k, tn)
    slab_specs = []
    for r in to_round:
        rows = r.shape[0] // n_steps
        assert rows * n_steps == r.shape[0] and rows % BF16_SUBLANES == 0
        slab_specs.append(pl.BlockSpec((rows, r.shape[1]), lambda j, i: (j * n_i + i, 0)))
    out = pl.pallas_call(
        _rounding_too(body, 2 + len(extra), len(to_round)),
        grid=(n // tn, n_i),
        in_specs=in_specs + list(extra_specs) + slab_specs,
        out_specs=[o_spec] + slab_specs,
        out_shape=[jax.ShapeDtypeStruct(o_shape, out_dtype)] + [jax.ShapeDtypeStruct(r.shape, BF16) for r in to_round],
        scratch_shapes=[pltpu.VMEM(w_block, BF16)] + list(scratch),
        compiler_params=_params("parallel", "arbitrary"),
        name=name,
    )(*args, *extra, *to_round)
    return out if to_round else out[0]


def _moba_kernel(*refs, n_cast):
    slopes_ref, qt_ref, k_ref, vt_ref = refs[:4]
    cast_in = refs[4:4 + n_cast]
    o_ref = refs[4 + n_cast]
    cast_out = refs[5 + n_cast:5 + 2 * n_cast]
    (kaug_ref, qaug_ref, kmh_ref, kml_ref, sel_ref, acc_ref, sd_ref, s0_ref, s1_ref,
     vaug_ref) = refs[5 + 2 * n_cast:]
    for src_ref, dst_ref in zip(cast_in, cast_out):
        dst_ref[...] = src_ref[...].astype(dst_ref.dtype)

    hg = pl.program_id(1)
    i = pl.program_id(2)
    blk = MOBA_BLOCK
    dh = ATTN_HEAD_DIM
    seq = k_ref.shape[0]
    nb = seq // blk
    n_chain = qt_ref.shape[0] // dh
    n_split = SLOPE_TERMS

    @pl.when(i == 0)
    def _():
        r = lax.broadcasted_iota(jnp.int32, (nb, seq), 0)
        c = lax.broadcasted_iota(jnp.int32, (nb, seq), 1)
        lo = r * blk
        pool = jnp.where((c >= lo) & (c < lo + blk), 1.0 / blk, 0.0).astype(BF16)
        k_lane = lax.broadcasted_iota(jnp.int32, (blk, AUG_COLS), 1)
        k_row = lax.broadcasted_iota(jnp.int32, (blk, AUG_COLS), 0).astype(F32)
        q_row = lax.broadcasted_iota(jnp.int32, (AUG_COLS, blk), 0)
        q_lane = lax.broadcasted_iota(jnp.int32, (AUG_COLS, blk), 1).astype(F32)
        ones_row = lax.broadcasted_iota(jnp.int32, (SUM_ROWS, seq), 0) == 0
        for g in range(n_chain):
            vaug_ref[g, 0:dh, :] = vt_ref[g * dh:(g + 1) * dh, :]
            vaug_ref[g, dh:dh + SUM_ROWS, :] = jnp.where(ones_row, 1.0, 0.0).astype(BF16)
            kg = k_ref[:, g * dh:(g + 1) * dh]
            kmean = _dot(pool, kg)
            hi = kmean.astype(BF16)
            kmh_ref[g] = hi
            kml_ref[g] = (kmean - hi.astype(F32)).astype(BF16)
            kaug_ref[g, :, 0:dh] = kg
            k_aug = jnp.where((k_lane >= n_split) & (k_lane < 2 * n_split), k_row, 0.0)
            q_aug = jnp.where(q_row < n_split, -q_lane, 0.0)
            for t in range(n_split):
                s_t = slopes_ref[(hg * n_chain + g) * n_split + t]
                k_aug = jnp.where(k_lane == t, s_t, k_aug)
                q_aug = jnp.where(q_row == n_split + t, s_t, q_aug)
            k_aug = k_aug.astype(BF16)
            for jb in range(nb):
                kaug_ref[g, jb * blk:(jb + 1) * blk, dh:dh + AUG_COLS] = k_aug
            qaug_ref[g, dh:dh + AUG_COLS, :] = q_aug.astype(BF16)

    chains = range(n_chain)

    def score_stage(j, buf_ref):
        jc = jnp.minimum(j, i)
        for g in chains:
            kb = kaug_ref[g, pl.ds(pl.multiple_of(jc * blk, blk), blk), :]
            buf_ref[g] = _dot(kb, qaug_ref[g])

    def values(g, j):
        jc = jnp.minimum(j, i)
        return vaug_ref[g, :, pl.ds(pl.multiple_of(jc * blk, blk), blk)]

    def softmax_stage(j, buf_ref, ms):
        far = ((i - j) * blk).astype(F32)
        ms_new = []
        for g in chains:
            m = ms[g]
            c_shift = slopes_ref[(ATTN_HEADS + hg * n_chain + g) * n_split] * far
            keep = sel_ref[g, pl.ds(jnp.minimum(j, nb - 1), 1), :] > 0.0
            m_new = jnp.maximum(m, jnp.where(keep, jnp.max(buf_ref[g], axis=0, keepdims=True) - c_shift, NEG_INF))
            alpha = jnp.exp2(m - m_new)
            p = jnp.exp2(buf_ref[g] - jnp.where(keep, m_new + c_shift, -NEG_INF))
            ms_new.append(m_new)
            acc_ref[g] = alpha * acc_ref[g] + _dot(values(g, j), p.astype(BF16))
        return tuple(ms_new)

    for g in chains:
        qaug_ref[g, 0:dh, :] = qt_ref[g * dh:(g + 1) * dh, :]
    score_stage(i, sd_ref)
    score_stage(0, s0_ref)

    jidx = lax.broadcasted_iota(jnp.int32, (nb, blk), 0)
    for g in chains:
        qt = qt_ref[g * dh:(g + 1) * dh, :]
        gate = _dot(kmh_ref[g], qt) + _dot(kml_ref[g], qt)
        cnt = jnp.zeros((nb, blk), F32)
        for jp in range(nb):
            row = gate[jp:jp + 1, :]
            beats = (row > gate) | ((row == gate) & (jidx > jp))
            cnt = cnt + jnp.where(beats, (jp < i).astype(F32), 0.0)
        sel_ref[g] = jnp.where((cnt < MOBA_TOPK) & (jidx < i), 1.0, 0.0)

    kl = lax.broadcasted_iota(jnp.int32, (blk, blk), 0)
    ql = lax.broadcasted_iota(jnp.int32, (blk, blk), 1)
    ms = []
    for g in chains:
        s = jnp.where(kl <= ql, sd_ref[g], NEG_INF)
        m0 = jnp.max(s, axis=0, keepdims=True)
        p = jnp.exp2(s - m0)
        ms.append(m0)
        acc_ref[g] = _dot(values(g, i), p.astype(BF16))

    def body(t, ms):
        j = 2 * t
        score_stage(j + 1, s1_ref)
        ms = softmax_stage(j, s0_ref, ms)
        score_stage(j + 2, s0_ref)
        return softmax_stage(j + 1, s1_ref, ms)

    ms = lax.fori_loop(0, i // 2, body, tuple(ms))

    @pl.when(i % 2 == 1)
    def _():
        softmax_stage(i - 1, s0_ref, ms)

    for g in chains:
        acc = acc_ref[g]
        o_ref[:, g * dh:(g + 1) * dh] = (acc[0:dh] / acc[dh:dh + 1]).T.astype(o_ref.dtype)


def _moba_attention(qt, k, vt, slope_table, bsz, seq, to_round=(), heads_per_step=4):
    assert seq % MOBA_BLOCK == 0 and ATTN_HEADS % heads_per_step == 0
    nq = seq // MOBA_BLOCK
    nb = nq
    dh = ATTN_HEAD_DIM
    g = heads_per_step
    n_hg = ATTN_HEADS // g
    n_steps = bsz * n_hg * nq
    slab_specs = []
    for w in to_round:
        rows = w.shape[0] // n_steps
        assert rows * n_steps == w.shape[0] and rows % BF16_SUBLANES == 0
        slab_specs.append(pl.BlockSpec((rows, w.shape[1]), lambda b, h, i: ((b * n_hg + h) * nq + i, 0)))
    return pl.pallas_call(
        functools.partial(_moba_kernel, n_cast=len(to_round)),
        grid=(bsz, n_hg, nq),
        in_specs=[pl.BlockSpec(memory_space=pltpu.SMEM),
                  pl.BlockSpec((g * dh, MOBA_BLOCK), lambda b, h, i: (h, b * nq + i)),
                  pl.BlockSpec((seq, g * dh), lambda b, h, i: (b, h)),
                  pl.BlockSpec((g * dh, seq), lambda b, h, i: (h, b))] + slab_specs,
        out_specs=[pl.BlockSpec((MOBA_BLOCK, g * dh), lambda b, h, i: (b * nq + i, h))] + slab_specs,
        out_shape=[jax.ShapeDtypeStruct((bsz * seq, ATTN_WIDTH), BF16)]
                  + [jax.ShapeDtypeStruct(w.shape, BF16) for w in to_round],
        scratch_shapes=[pltpu.VMEM((g, seq, dh + AUG_COLS), BF16),
                        pltpu.VMEM((g, dh + AUG_COLS, MOBA_BLOCK), BF16),
                        pltpu.VMEM((g, nb, dh), BF16), pltpu.VMEM((g, nb, dh), BF16),
                        pltpu.VMEM((g, nb, MOBA_BLOCK), F32),
                        pltpu.VMEM((g, dh + SUM_ROWS, MOBA_BLOCK), F32)]
                       + [pltpu.VMEM((g, MOBA_BLOCK, MOBA_BLOCK), F32)] * 3
                       + [pltpu.VMEM((g, dh + SUM_ROWS, seq), BF16)],
        compiler_params=_params("parallel", "parallel", "arbitrary"),
        name="moba_attention",
    )(slope_table, qt, k, vt, *to_round)


def _slope_table():
    slopes = jnp.exp2(-8.0 * jnp.arange(1, ATTN_HEADS + 1, dtype=F32) / ATTN_HEADS) * LOG2E
    terms, rest = [], slopes
    for _ in range(SLOPE_TERMS):
        t = rest.astype(BF16).astype(F32)
        terms.append(t)
        rest = rest - t
    split = jnp.stack(terms, axis=1).reshape(-1)
    full = jnp.stack([slopes] * SLOPE_TERMS, axis=1).reshape(-1)
    return jnp.concatenate([split, full])


def _split3(x):
    h1 = x.astype(BF16)
    r1 = x - h1.astype(F32)
    h2 = r1.astype(BF16)
    h3 = (r1 - h2.astype(F32)).astype(BF16)
    return h1, h2, h3


def _softplus(x):
    return jnp.maximum(x, 0.0) + jnp.log1p(jnp.exp(-jnp.abs(x)))


def _ssd_kernel(act_ref, z_ref, dt_ref, dtt_ref, dtb_ref, dtbt_ref, alog_ref, alogt_ref,
                dskip_ref, normw_ref, o_ref, state_ref, y_ref):
    c = pl.program_id(1)
    ln = SSM_CHUNK

    @pl.when(c == 0)
    def _():
        state_ref[...] = jnp.zeros_like(state_ref)

    row = lax.broadcasted_iota(jnp.int32, (ln, ln), 0)
    col = lax.broadcasted_iota(jnp.int32, (ln, ln), 1)
    causal = row >= col
    tri = jnp.where(causal, 1.0, 0.0).astype(BF16)
    trit = jnp.where(row <= col, 1.0, 0.0).astype(BF16)
    left = lax.broadcasted_iota(jnp.int32, (1, LANES), 1) < SSM_HEAD_DIM
    left_full = lax.broadcasted_iota(jnp.int32, (ln, LANES), 1) < SSM_HEAD_DIM
    pairs_per_group = GROUP_WIDTH // LANES

    for cc in range(act_ref.shape[0] // ln):
        rows = slice(cc * ln, (cc + 1) * ln)
        dt = _softplus(dt_ref[rows, :] + dtb_ref[...])
        dtt = _softplus(dtt_ref[:, rows] + dtbt_ref[...])
        la = dt * (-jnp.exp(alog_ref[...]))
        lat = dtt * (-jnp.exp(alogt_ref[...]))
        a1, a2, a3 = _split3(la)
        acum = (_dot(tri, a1) + _dot(tri, a2) + _dot(tri, a3)) * LOG2E
        b1, b2, b3 = _split3(lat)
        acumt = (_dot(b1, trit) + _dot(b2, trit) + _dot(b3, trit)) * LOG2E
        rowdt = acumt - jnp.log2(dtt)

        for g in range(SSM_GROUPS):
            b0 = SSM_INNER + g * SSM_STATE
            c0 = SSM_INNER + SSM_BC + g * SSM_STATE
            bgb = act_ref[rows, b0:b0 + SSM_STATE]
            cgb = act_ref[rows, c0:c0 + SSM_STATE]
            cb = _dot_nt(cgb, bgb)
            bgt = bgb.astype(F32).T
            for pp in range(pairs_per_group):
                p = g * pairs_per_group + pp
                x2 = act_ref[rows, p * LANES:(p + 1) * LANES]
                x_heads = (jnp.where(left_full, x2, 0), jnp.where(left_full, 0, x2))
                ops = []
                eacs = []
                for hh in (2 * p, 2 * p + 1):
                    lhs = []
                    colb = jnp.broadcast_to(acum[:, hh:hh + 1], (ln, ln))
                    rowb = rowdt[hh:hh + 1, :]
                    decay_dt = jnp.exp2(jnp.where(causal, colb - rowb, -jnp.inf))
                    lhs.append((cb * decay_dt).astype(BF16))
                    wt = jnp.exp2(acumt[hh:hh + 1, ln - 1:ln] - rowb)
                    lhs.append((bgt * wt).astype(BF16))
                    eacs.append(jnp.exp2(colb))
                    ops.append(jnp.concatenate(lhs, axis=0))
                res = _dot(ops[0], x_heads[0]) + _dot(ops[1], x_heads[1])
                eac = jnp.where(left, eacs[0], eacs[1])
                st = state_ref[p]
                y = res[0:ln] + _dot(cgb, st.astype(BF16)) * eac
                y_ref[:, p * LANES:(p + 1) * LANES] = y
                state_ref[p] = st * eac[ln - 1:ln, :] + res[ln:2 * ln]
            g0 = g * GROUP_WIDTH
            xg = act_ref[rows, g0:g0 + GROUP_WIDTH].astype(F32)
            yg = y_ref[:, g0:g0 + GROUP_WIDTH] + xg * dskip_ref[:, g0:g0 + GROUP_WIDTH]
            yg = yg * z_ref[rows, g0:g0 + GROUP_WIDTH].astype(F32)
            ms = jnp.mean(yg * yg, axis=-1, keepdims=True)
            o_ref[rows, g0:g0 + GROUP_WIDTH] = (yg * lax.rsqrt(ms + EPS)
                                                * normw_ref[:, g0:g0 + GROUP_WIDTH]).astype(o_ref.dtype)


def _ssd_branch(act, zact, dt, dtt, dt_bias, a_log, d_skip, norm_w, bsz, seq):
    ln = SSM_CHUNK * SSD_CHUNKS_PER_STEP
    assert seq % ln == 0
    nc = seq // ln
    hs = SSM_HEADS
    row = lambda b, c: (b * nc + c, 0)
    fixed = lambda b, c: (0, 0)
    return pl.pallas_call(
        _ssd_kernel,
        grid=(bsz, nc),
        in_specs=[pl.BlockSpec((ln, SSM_CONV_DIM), row),
                  pl.BlockSpec((ln, SSM_INNER), row),
                  pl.BlockSpec((ln, hs), row),
                  pl.BlockSpec((hs, ln), lambda b, c: (0, b * nc + c)),
                  pl.BlockSpec((1, hs), fixed),
                  pl.BlockSpec((hs, 1), fixed),
                  pl.BlockSpec((1, hs), fixed),
                  pl.BlockSpec((hs, 1), fixed),
                  pl.BlockSpec((1, SSM_INNER), fixed),
                  pl.BlockSpec((1, SSM_INNER), fixed)],
        out_specs=pl.BlockSpec((ln, SSM_INNER), row),
        out_shape=jax.ShapeDtypeStruct((bsz * seq, SSM_INNER), BF16),
        scratch_shapes=[pltpu.VMEM((SSM_HEADS // 2, SSM_STATE, LANES), F32),
                        pltpu.VMEM((SSM_CHUNK, SSM_INNER), F32)],
        compiler_params=_params("arbitrary", "arbitrary"),
        name="ssd",
    )(act, zact, dt, dtt, dt_bias.reshape(1, hs), dt_bias.reshape(hs, 1),
      a_log.reshape(1, hs), a_log.reshape(hs, 1),
      jnp.repeat(d_skip, SSM_HEAD_DIM).reshape(1, SSM_INNER), norm_w.reshape(1, SSM_INNER))


def _merge_kernel(attn_ref, ssm_ref, ga_ref, gs_ref, wa_ref, ws_ref, o_ref):
    ya = _dot(attn_ref[...], wa_ref[...])
    ys = _dot(ssm_ref[...], ws_ref[...])
    o_ref[...] = (ga_ref[...].astype(F32) * ya + gs_ref[...].astype(F32) * ys).astype(o_ref.dtype)


def _merge(attn, ssm, gate, wa, ws, tm=1024, tn=512):
    t = attn.shape[0]
    n = wa.shape[1]
    nj = n // tn
    return pl.pallas_call(
        _merge_kernel,
        grid=(t // tm, nj),
        in_specs=[pl.BlockSpec((tm, attn.shape[1]), lambda i, j: (i, 0)),
                  pl.BlockSpec((tm, ssm.shape[1]), lambda i, j: (i, 0)),
                  pl.BlockSpec((tm, tn), lambda i, j: (i, j)),
                  pl.BlockSpec((tm, tn), lambda i, j: (i, j + nj)),
                  pl.BlockSpec((wa.shape[0], tn), lambda i, j: (0, j)),
                  pl.BlockSpec((ws.shape[0], tn), lambda i, j: (0, j))],
        out_specs=pl.BlockSpec((tm, tn), lambda i, j: (i, j)),
        out_shape=jax.ShapeDtypeStruct((t, n), BF16),
        compiler_params=_params("parallel", "parallel"),
        name="merge",
    )(attn, ssm, gate, gate, wa, ws)


def _out_mlp_kernel(x_ref, m_ref, wo_ref, nw_ref, wu_ref, wd_ref, o_ref, hn_ref):
    j = pl.program_id(1)

    @pl.when(j == 0)
    def _():
        h = x_ref[...] + _dot(m_ref[...], wo_ref[...])
        ms = jnp.mean(h * h, axis=-1, keepdims=True)
        hn_ref[...] = (h * lax.rsqrt(ms + EPS) * nw_ref[...]).astype(hn_ref.dtype)
        o_ref[...] = h

    u = _dot(hn_ref[...], wu_ref[...])
    act = jnp.square(jnp.maximum(u, 0.0)).astype(BF16)
    o_ref[...] += _dot(act, wd_ref[...])


def _out_mlp(x, merged, w_out, norm_w, w_up, w_down, tm=512, tf=1024):
    t, d = x.shape
    f = w_up.shape[1]
    return pl.pallas_call(
        _out_mlp_kernel,
        grid=(t // tm, f // tf),
        in_specs=[pl.BlockSpec((tm, d), lambda i, j: (i, 0)),
                  pl.BlockSpec((tm, merged.shape[1]), lambda i, j: (i, 0)),
                  pl.BlockSpec(w_out.shape, lambda i, j: (0, 0), pipeline_mode=pl.Buffered(1)),
                  pl.BlockSpec((1, d), lambda i, j: (0, 0)),
                  pl.BlockSpec((d, tf), lambda i, j: (0, j)),
                  pl.BlockSpec((tf, d), lambda i, j: (j, 0))],
        out_specs=pl.BlockSpec((tm, d), lambda i, j: (i, 0)),
        out_shape=jax.ShapeDtypeStruct((t, d), F32),
        scratch_shapes=[pltpu.VMEM((tm, d), BF16)],
        compiler_params=_params("parallel", "arbitrary"),
        name="out_mlp",
    )(x, merged, w_out, norm_w.reshape(1, d), w_up, w_down)


def _layer(h, bsz, seq, mix_norm_w, w_in, q_norm_w, k_norm_w, conv_w, conv_b, dt_bias, a_log, d_skip,
           ssm_norm_w, w_attn_out, w_ssm_out, w_out, mlp_norm_w, w_up, w_down, slopes):
    o_k = ATTN_WIDTH
    o_v = 2 * ATTN_WIDTH
    o_z = 3 * ATTN_WIDTH
    o_x = o_z + SSM_INNER
    o_dt = o_x + SSM_CONV_DIM
    o_g = o_dt + SSM_HEADS
    assert o_g - o_dt == GATE_SKEW and o_dt % PROJ_TN == 0
    w_t = jnp.swapaxes(w_in, 0, 1)
    wdt_t = w_t[o_dt:o_g].astype(BF16)
    head_col = pl.BlockSpec((ATTN_HEAD_DIM, 1), lambda j, i: (0, 0))
    head_row = pl.BlockSpec((1, ATTN_HEAD_DIM), lambda j, i: (0, 0))
    tn = PROJ_TN
    skew_blocks = tn // GATE_SKEW

    hn, dt, dtt = _rmsnorm_dt(h, mix_norm_w, wdt_t.T, wdt_t)
    qt = _project(hn, w_t, _proj_qnorm_t_kernel, n=ATTN_WIDTH, transposed=True,
                  extra=(q_norm_w.reshape(ATTN_HEAD_DIM, 1),), extra_specs=(head_col,), name="proj_q")
    k = _project(hn, w_t, _proj_knorm_kernel, n=ATTN_WIDTH, col0=o_k,
                 extra=(k_norm_w.reshape(1, ATTN_HEAD_DIM),), extra_specs=(head_row,), name="proj_k")
    vt, wo_bf, wa_bf, ws_bf = _project(hn, w_t, _proj_t_kernel, n=ATTN_WIDTH, col0=o_v, transposed=True,
                                       to_round=(w_out, w_attn_out, w_ssm_out), name="proj_v")
    zact, wu_bf = _project(hn, w_t, functools.partial(_proj_act_kernel, act=_silu), n=SSM_INNER, col0=o_z,
                           to_round=(w_up,), name="proj_z")
    act = _project(hn, w_t, functools.partial(_proj_conv_kernel, seq=seq, sub=CONV_SUB), n=SSM_CONV_DIM, col0=o_x,
                   extra=(conv_w, conv_b.reshape(1, -1)),
                   extra_specs=(pl.BlockSpec((SSM_CONV, tn), lambda j, i: (0, j)),
                                pl.BlockSpec((1, tn), lambda j, i: (0, j))),
                   scratch=(pltpu.VMEM((CONV_HALO, tn), F32),), name="proj_xbc")
    gates, wd_bf = _project(hn, w_t, _proj_gate_kernel, n=N_GATES, col0=o_dt, extra=(w_t,),
                     extra_specs=(pl.BlockSpec((GATE_SKEW, w_t.shape[1]),
                                               lambda j, i: (o_dt // GATE_SKEW + (j + 1) * skew_blocks, 0)),),
                     to_round=(w_down,), name="proj_gate")

    attn = _moba_attention(qt, k, vt, slopes, bsz, seq)[0]
    ssm = _ssd_branch(act, zact, dt, dtt, dt_bias, a_log, d_skip, ssm_norm_w, bsz, seq)
    merged = _merge(attn, ssm, gates, wa_bf, ws_bf)
    return _out_mlp(h, merged, wo_bf, mlp_norm_w, wu_bf, wd_bf)


def kernel(x, mix_norm_w, w_in, q_norm_w, k_norm_w, conv_w, conv_b, dt_bias, a_log, d_skip, ssm_norm_w,
           w_attn_out, w_ssm_out, w_out, mlp_norm_w, w_up, w_down):
    bsz, seq, d = x.shape
    slopes = _slope_table()
    h = x.reshape(bsz * seq, d)
    for layer in range(w_in.shape[0]):
        h = _layer(h, bsz, seq, mix_norm_w[layer], w_in[layer], q_norm_w[layer], k_norm_w[layer], conv_w[layer],
                   conv_b[layer], dt_bias[layer], a_log[layer], d_skip[layer], ssm_norm_w[layer], w_attn_out[layer],
                   w_ssm_out[layer], w_out[layer], mlp_norm_w[layer], w_up[layer], w_down[layer], slopes)
    return h.reshape(bsz, seq, d)
```
